```python
import math
import jax, jax.numpy as jnp
from jax import lax
import numpy as np

D_MODEL = 2048
BATCH = 4
SEQ = 4096
DEPTH = 2

SSD_D_INNER = D_MODEL
SSD_HEAD_DIM = 64
SSD_HEADS = SSD_D_INNER // SSD_HEAD_DIM
SSD_GROUPS = 4
SSD_STATE = 128
SSD_CONV = 5
SSD_CHUNK = 128
SSD_XBC = SSD_D_INNER + 2 * SSD_GROUPS * SSD_STATE

ATTN_HEAD_DIM = 128
ATTN_PATTERNS = ((128, 1), (512, 4), (2048, 16))
ATTN_HEADS_PER_GROUP = 4
ATTN_HEADS = ATTN_HEADS_PER_GROUP * len(ATTN_PATTERNS)
ATTN_WIDTH = ATTN_HEADS * ATTN_HEAD_DIM
ATTN_OUT = ATTN_HEADS_PER_GROUP * ATTN_HEAD_DIM
ALIBI_MAX_EXP = 8.0
NEG_INF = -1e30

CONV_CHANNELS = D_MODEL
CONV_WIDTH = 31

MLP_HIDDEN = 4 * D_MODEL

N_BRANCHES = 3
NORM_EPS = 1e-6

COL_Z = 0
COL_XBC = COL_Z + SSD_D_INNER
COL_DT = COL_XBC + SSD_XBC
COL_Q = COL_DT + 2 * SSD_HEADS
COL_K = COL_Q + ATTN_WIDTH
COL_V = COL_K + ATTN_WIDTH
COL_GLU = COL_V + ATTN_WIDTH
COL_GATE = COL_GLU + 2 * CONV_CHANNELS
N_IN = COL_GATE + N_BRANCHES * D_MODEL

kernel_name = "hybrid_gated_ssd_dilattn_conformer_encoder"


def rms_norm(x, g):
    xf = x.astype(jnp.float32)
    y = xf * lax.rsqrt(jnp.mean(xf * xf, axis=-1, keepdims=True) + NORM_EPS)
    return y.astype(x.dtype) * g


def layer_norm(x, g, b):
    xf = x.astype(jnp.float32)
    mu = jnp.mean(xf, axis=-1, keepdims=True)
    var = jnp.mean(jnp.square(xf - mu), axis=-1, keepdims=True)
    y = (xf - mu) * lax.rsqrt(var + NORM_EPS)
    return y.astype(x.dtype) * g + b


def depthwise_conv(x, w, b):
    k = w.shape[0]
    y = lax.conv_general_dilated(
        x, w[:, None, :], window_strides=(1,), padding=[(k // 2, k // 2)],
        dimension_numbers=("NWC", "WIO", "NWC"), feature_group_count=x.shape[-1])
    return y + b


def ssd_chunked(xh, dt, a, bm, cm):
    bsz, seqlen, nh, hp = xh.shape
    ng, ns = bm.shape[2], bm.shape[3]
    nr = nh // ng
    q = SSD_CHUNK
    nc = seqlen // q
    xdt = (xh * dt[..., None]).reshape(bsz, nc, q, ng, nr, hp)
    adt = (dt * a).reshape(bsz, nc, q, ng, nr).transpose(0, 3, 4, 1, 2)
    a_cs = jnp.cumsum(adt, axis=-1)
    seg = a_cs[..., :, None] - a_cs[..., None, :]
    lower = jnp.tril(jnp.ones((q, q), dtype=bool))
    lmat = jnp.exp(jnp.where(lower, seg, -jnp.inf))
    bc = bm.reshape(bsz, nc, q, ng, ns)
    cc = cm.reshape(bsz, nc, q, ng, ns)
    cb = jnp.einsum("bclgn,bcsgn->bgcls", cc, bc)
    y_diag = jnp.einsum("bgrcls,bcsgrp->bclgrp", cb[:, :, None] * lmat, xdt)
    decay_states = jnp.exp(a_cs[..., -1:] - a_cs)
    states = jnp.einsum("bcsgn,bgrcs,bcsgrp->cbgrpn", bc, decay_states, xdt)
    chunk_decay = jnp.exp(a_cs[..., -1]).transpose(3, 0, 1, 2)

    def step(h, inp):
        dec, st = inp
        return dec[..., None, None] * h + st, h

    h0 = jnp.zeros(states.shape[1:], states.dtype)
    _, h_in = lax.scan(step, h0, (chunk_decay, states))
    y_off = jnp.einsum("bclgn,cbgrpn,bgrcl->bclgrp", cc, h_in, jnp.exp(a_cs))
    return (y_diag + y_off).reshape(bsz, seqlen, nh, hp)


def ssd_mixer(z, xbc, dt_raw, conv_w, conv_b, dt_bias, a_log, d_skip, norm_g):
    bsz, seqlen = z.shape[:2]
    xbc = jax.nn.silu(depthwise_conv(xbc, conv_w, conv_b))
    xs = xbc[..., :SSD_D_INNER]
    bm = xbc[..., SSD_D_INNER:SSD_D_INNER + SSD_GROUPS * SSD_STATE]
    cm = xbc[..., SSD_D_INNER + SSD_GROUPS * SSD_STATE:]
    xh = xs.reshape(bsz, seqlen, SSD_HEADS, SSD_HEAD_DIM)
    bm = bm.reshape(bsz, seqlen, SSD_GROUPS, SSD_STATE)
    cm = cm.reshape(bsz, seqlen, SSD_GROUPS, SSD_STATE)
    a = -jnp.exp(a_log)
    dt = jax.nn.softplus(dt_raw.reshape(bsz, seqlen, 2, SSD_HEADS) + dt_bias)
    flip = lambda t: jnp.flip(t, axis=1)
    y_fwd = ssd_chunked(xh, dt[:, :, 0], a[0], bm, cm)
    y_bwd = flip(ssd_chunked(flip(xh), flip(dt[:, :, 1]), a[1], flip(bm), flip(cm)))
    y = y_fwd + y_bwd + d_skip[:, None] * xh
    y = y.reshape(bsz, seqlen, SSD_D_INNER) * jax.nn.silu(z)
    return rms_norm(y, norm_g)


def alibi_slopes():
    return jnp.exp2(-ALIBI_MAX_EXP * jnp.arange(1, ATTN_HEADS + 1, dtype=jnp.float32) / ATTN_HEADS)


def dilated_window_attention(q, k, v, slopes, window, dilation):
    bsz, seqlen, nh, hd = q.shape
    radius = window // (2 * dilation)
    blk = radius
    sub = seqlen // dilation
    nb = -(-sub // blk)
    lp = nb * blk

    def to_sub(t):
        t = t.reshape(bsz, sub, dilation, nh, hd).transpose(0, 2, 1, 3, 4)
        return jnp.pad(t, ((0, 0), (0, 0), (0, lp - sub), (0, 0), (0, 0)))

    def neighbours(t):
        t = jnp.pad(to_sub(t), ((0, 0), (0, 0), (blk, blk), (0, 0), (0, 0)))
        t = t.reshape(bsz, dilation, nb + 2, blk, nh, hd)
        return jnp.concatenate([t[:, :, :-2], t[:, :, 1:-1], t[:, :, 2:]], axis=3)

    qs = to_sub(q).reshape(bsz, dilation, nb, blk, nh, hd)
    ks, vs = neighbours(k), neighbours(v)
    qi = jnp.arange(nb)[:, None] * blk + jnp.arange(blk)[None, :]
    kj = (jnp.arange(nb)[:, None] - 1) * blk + jnp.arange(3 * blk)[None, :]
    rel = kj[:, None, :] - qi[:, :, None]
    valid = (jnp.abs(rel) <= radius) & (kj[:, None, :] >= 0) & (kj[:, None, :] < sub)
    dist = (dilation * jnp.abs(rel)).astype(jnp.float32)
    s = jnp.einsum("brnqhe,brnkhe->brnhqk", qs, ks).astype(jnp.float32) * (hd ** -0.5)
    s = s - slopes[:, None, None] * dist[:, None]
    s = jnp.where(valid[:, None], s, NEG_INF)
    lse = jax.nn.logsumexp(s, axis=-1)
    p = jnp.exp(s - lse[..., None])
    o = jnp.einsum("brnhqk,brnkhe->brnqhe", p.astype(v.dtype), vs)

    def from_sub(t):
        t = t.reshape((bsz, dilation, lp) + t.shape[4:])[:, :, :sub]
        return jnp.swapaxes(t, 1, 2).reshape((bsz, seqlen) + t.shape[3:])

    return from_sub(o), from_sub(jnp.moveaxis(lse, 3, 4))


def dilated_attention_mixer(q, k, v, q_g, k_g):
    bsz, seqlen = q.shape[:2]
    shape = (bsz, seqlen, ATTN_HEADS, ATTN_HEAD_DIM)
    q = rms_norm(q.reshape(shape), q_g)
    k = rms_norm(k.reshape(shape), k_g)
    v = v.reshape(shape)
    slopes = alibi_slopes()
    outs, lses = [], []
    for g, (window, dilation) in enumerate(ATTN_PATTERNS):
        sl = slice(g * ATTN_HEADS_PER_GROUP, (g + 1) * ATTN_HEADS_PER_GROUP)
        o, l = dilated_window_attention(q[:, :, sl], k[:, :, sl], v[:, :, sl], slopes[sl], window, dilation)
        outs.append(o)
        lses.append(l)
    wts = jax.nn.softmax(jnp.stack(lses, axis=0), axis=0)
    o = jnp.sum(wts[..., None].astype(v.dtype) * jnp.stack(outs, axis=0), axis=0)
    return o.reshape(bsz, seqlen, ATTN_OUT)


def conformer_conv(u, dw_w, dw_b, ln_g, ln_b):
    a = u[..., :CONV_CHANNELS]
    gate = u[..., CONV_CHANNELS:]
    y = a * jax.nn.sigmoid(gate)
    y = depthwise_conv(y, dw_w, dw_b)
    y = layer_norm(y, ln_g, ln_b)
    return jax.nn.silu(y)


def setup_inputs(seed: int = 0) -> dict:
    key = jax.random.key(seed)
    ks = jax.random.split(key, 24)
    L = DEPTH

    def nrm(k, shape, scale):
        return jax.random.normal(k, shape, jnp.float32) * scale

    def gain(k, shape, scale=0.02):
        return 1.0 + scale * jax.random.normal(k, shape, jnp.float32)

    dt0 = jnp.exp(jax.random.uniform(ks[5], (L, 2, SSD_HEADS), jnp.float32,
                                     math.log(1e-3), math.log(1e-1)))
    return {
        "x": nrm(ks[0], (BATCH, SEQ, D_MODEL), 1.0),
        "norm1_g": gain(ks[1], (L, D_MODEL)),
        "w_in": nrm(ks[2], (L, D_MODEL, N_IN), D_MODEL ** -0.5),
        "ssd_conv_w": nrm(ks[3], (L, SSD_CONV, SSD_XBC), SSD_CONV ** -0.5),
        "ssd_conv_b": nrm(ks[4], (L, SSD_XBC), 0.02),
        "ssd_dt_bias": dt0 + jnp.log(-jnp.expm1(-dt0)),
        "ssd_a_log": jnp.log(jax.random.uniform(ks[6], (L, 2, SSD_HEADS), jnp.float32, 1.0, 16.0)),
        "ssd_d": gain(ks[7], (L, SSD_HEADS), 0.1),
        "ssd_norm_g": gain(ks[8], (L, SSD_D_INNER)),
        "w_ssd_o": nrm(ks[9], (L, SSD_D_INNER, D_MODEL), SSD_D_INNER ** -0.5),
        "q_norm_g": gain(ks[10], (L, ATTN_HEAD_DIM)),
        "k_norm_g": gain(ks[11], (L, ATTN_HEAD_DIM)),
        "w_attn_o": nrm(ks[12], (L, ATTN_OUT, D_MODEL), ATTN_OUT ** -0.5),
        "conv_dw_w": nrm(ks[13], (L, CONV_WIDTH, CONV_CHANNELS), CONV_WIDTH ** -0.5),
        "conv_dw_b": nrm(ks[14], (L, CONV_CHANNELS), 0.02),
        "conv_ln_g": gain(ks[15], (L, CONV_CHANNELS)),
        "conv_ln_b": nrm(ks[16], (L, CONV_CHANNELS), 0.02),
        "w_conv_o": nrm(ks[17], (L, CONV_CHANNELS, D_MODEL), CONV_CHANNELS ** -0.5),
        "w_out": nrm(ks[18], (L, D_MODEL, D_MODEL), D_MODEL ** -0.5),
        "norm2_g": gain(ks[19], (L, D_MODEL)),
        "w_mlp_up": nrm(ks[20], (L, D_MODEL, MLP_HIDDEN), D_MODEL ** -0.5),
        "w_mlp_down": nrm(ks[21], (L, MLP_HIDDEN, D_MODEL), MLP_HIDDEN ** -0.5),
    }


def reference(x, norm1_g, w_in, ssd_conv_w, ssd_conv_b, ssd_dt_bias, ssd_a_log, ssd_d,
              ssd_norm_g, w_ssd_o, q_norm_g, k_norm_g, w_attn_o, conv_dw_w, conv_dw_b,
              conv_ln_g, conv_ln_b, w_conv_o, w_out, norm2_g, w_mlp_up, w_mlp_down):
    bsz, seqlen = x.shape[:2]
    for i in range(DEPTH):
        h = rms_norm(x, norm1_g[i])
        proj = h @ w_in[i]
        y_ssd = ssd_mixer(proj[..., COL_Z:COL_XBC], proj[..., COL_XBC:COL_DT],
                          proj[..., COL_DT:COL_Q], ssd_conv_w[i], ssd_conv_b[i],
                          ssd_dt_bias[i], ssd_a_log[i], ssd_d[i], ssd_norm_g[i]) @ w_ssd_o[i]
        y_att = dilated_attention_mixer(proj[..., COL_Q:COL_K], proj[..., COL_K:COL_V],
                                        proj[..., COL_V:COL_GLU], q_norm_g[i], k_norm_g[i]) @ w_attn_o[i]
        y_conv = conformer_conv(proj[..., COL_GLU:COL_GATE], conv_dw_w[i], conv_dw_b[i],
                                conv_ln_g[i], conv_ln_b[i]) @ w_conv_o[i]
        gates = jax.nn.sigmoid(proj[..., COL_GATE:]).reshape(bsz, seqlen, N_BRANCHES, D_MODEL)
        merged = gates[:, :, 0] * y_ssd + gates[:, :, 1] * y_att + gates[:, :, 2] * y_conv
        x = x + merged @ w_out[i]
        h2 = rms_norm(x, norm2_g[i])
        x = x + jnp.square(jax.nn.relu(h2 @ w_mlp_up[i])) @ w_mlp_down[i]
    return x
```

```python
import functools
import math

import jax
import jax.numpy as jnp
from jax import lax
from jax.experimental import pallas as pl
from jax.experimental.pallas import tpu as pltpu

F32 = jnp.float32
BF16 = jnp.bfloat16

D_MODEL = 2048
SSD_HEAD_DIM = 64
SSD_HEADS = D_MODEL // SSD_HEAD_DIM
SSD_GROUPS = 4
SSD_STATE = 128
SSD_CHUNK = 128
SSD_BC = SSD_GROUPS * SSD_STATE
SSD_XBC = D_MODEL + 2 * SSD_BC
ATTN_HEAD_DIM = 128
ATTN_PATTERNS = ((128, 1), (512, 4), (2048, 16))
ATTN_HPG = 4
ATTN_HEADS = ATTN_HPG * len(ATTN_PATTERNS)
ATTN_WIDTH = ATTN_HEADS * ATTN_HEAD_DIM
ATTN_OUT = ATTN_HPG * ATTN_HEAD_DIM
ALIBI_MAX_EXP = 8.0
NEG_INF = -1e30
CONV_WIDTH = 31
MLP_HIDDEN = 4 * D_MODEL
NORM_EPS = 1e-6

COL_Z = 0
COL_XBC = COL_Z + D_MODEL
COL_DT = COL_XBC + SSD_XBC
COL_Q = COL_DT + 2 * SSD_HEADS
COL_GLU = COL_Q + 3 * ATTN_WIDTH
COL_GATE = COL_GLU + 2 * D_MODEL

LANES = 128
HALO = 16
VMEM_LIMIT = 56 * 1024 * 1024


def _params(sem):
    return pltpu.CompilerParams(dimension_semantics=sem, vmem_limit_bytes=VMEM_LIMIT)


def _rms(x, g):
    return x * lax.rsqrt(jnp.mean(x * x, axis=-1, keepdims=True) + NORM_EPS) * g


def _sigmoid(x):
    return 1.0 / (1.0 + jnp.exp(-x))


def _norm_kernel(x_ref, g_ref, o_ref):
    o_ref[...] = _rms(x_ref[...], g_ref[...]).astype(o_ref.dtype)


def _norm(x, g, tm=512):
    m, d = x.shape
    return pl.pallas_call(
        _norm_kernel,
        grid=(m // tm,),
        in_specs=[pl.BlockSpec((tm, d), lambda i: (i, 0)), pl.BlockSpec((1, d), lambda i: (0, 0))],
        out_specs=pl.BlockSpec((tm, d), lambda i: (i, 0)),
        out_shape=jax.ShapeDtypeStruct((m, d), BF16),
        compiler_params=_params(("parallel",)),
        name="rmsnorm",
    )(x, g.reshape(1, d))


def _mm_kernel(*refs, nw, na, epilogue):
    a_ref = refs[0]
    w_refs = refs[1:1 + nw]
    aux_refs = refs[1 + nw:1 + nw + na]
    o_ref = refs[-1]
    a = a_ref[...]
    accs = [jnp.dot(a, w[...], preferred_element_type=F32) for w in w_refs]
    o_ref[...] = epilogue(accs, [r[...] for r in aux_refs], pl.program_id(1)).astype(o_ref.dtype)


def _mm(a, ws, auxs, epilogue, out_dtype, tm, tn, name):
    m, k = a.shape
    n = ws[0].shape[1]
    in_specs = [pl.BlockSpec((tm, k), lambda i, j: (i, 0))]
    in_specs += [pl.BlockSpec((k, tn), lambda i, j: (0, j)) for _ in ws]
    in_specs += [pl.BlockSpec((1, tn), lambda i, j: (0, j)) for _ in auxs]
    return pl.pallas_call(
        functools.partial(_mm_kernel, nw=len(ws), na=len(auxs), epilogue=epilogue),
        grid=(m // tm, n // tn),
        in_specs=in_specs,
        out_specs=pl.BlockSpec((tm, tn), lambda i, j: (i, j)),
        out_shape=jax.ShapeDtypeStruct((m, n), out_dtype),
        compiler_params=_params(("parallel", "arbitrary")),
        name=name,
    )(a, *ws, *auxs)


def _epi_silu(accs, auxs, j):
    z = accs[0]
    return z * _sigmoid(z)


def _epi_id(accs, auxs, j):
    return accs[0]


def _epi_sigmoid(accs, auxs, j):
    return _sigmoid(accs[0])


def _epi_glu(accs, auxs, j):
    return accs[0] * _sigmoid(accs[1])


def _epi_qkv(accs, auxs, j, *, n_norm_tiles):
    acc, gain = accs[0], auxs[0]
    heads = []
    for hh in range(acc.shape[1] // ATTN_HEAD_DIM):
        sl = slice(hh * ATTN_HEAD_DIM, (hh + 1) * ATTN_HEAD_DIM)
        heads.append(_rms(acc[:, sl], gain[:, sl]))
    normed = jnp.concatenate(heads, axis=1)
    return jnp.where(j < n_norm_tiles, normed, acc)


def _dt_kernel(h_ref, w_ref, wt_ref, b_ref, bt_ref, dt_ref, dtt_ref):
    h = h_ref[...]
    raw = jnp.dot(h, w_ref[...], preferred_element_type=F32) + b_ref[...]
    sp = jnp.maximum(raw, 0.0) + jnp.log1p(jnp.exp(-jnp.abs(raw)))
    dt_ref[0] = sp[:, :SSD_HEADS]
    dt_ref[1] = sp[:, SSD_HEADS:]
    raw_t = lax.dot_general(wt_ref[...], h, (((1,), (1,)), ((), ())),
                            preferred_element_type=F32) + bt_ref[...]
    sp_t = jnp.maximum(raw_t, 0.0) + jnp.log1p(jnp.exp(-jnp.abs(raw_t)))
    for c in range(sp_t.shape[1] // SSD_CHUNK):
        blk = sp_t[:, c * SSD_CHUNK:(c + 1) * SSD_CHUNK]
        dtt_ref[c, 0] = blk[:SSD_HEADS]
        dtt_ref[c, 1] = blk[SSD_HEADS:]


def _dt_proj(h, w_dt, bias, tm=512):
    m, k = h.shape
    nd = 2 * SSD_HEADS
    cpb = tm // SSD_CHUNK
    return pl.pallas_call(
        _dt_kernel,
        grid=(m // tm,),
        in_specs=[pl.BlockSpec((tm, k), lambda i: (i, 0)),
                  pl.BlockSpec((k, nd), lambda i: (0, 0)),
                  pl.BlockSpec((nd, k), lambda i: (0, 0)),
                  pl.BlockSpec((1, nd), lambda i: (0, 0)),
                  pl.BlockSpec((nd, 1), lambda i: (0, 0))],
        out_specs=[pl.BlockSpec((2, tm, SSD_HEADS), lambda i: (0, i, 0)),
                   pl.BlockSpec((cpb, 2, SSD_HEADS, SSD_CHUNK), lambda i: (i, 0, 0, 0))],
        out_shape=[jax.ShapeDtypeStruct((2, m, SSD_HEADS), F32),
                   jax.ShapeDtypeStruct((m // SSD_CHUNK, 2, SSD_HEADS, SSD_CHUNK), F32)],
        compiler_params=_params(("parallel",)),
        name="dt_proj",
    )(h, w_dt, w_dt.T, bias.reshape(1, nd), bias.reshape(nd, 1))


def _dwconv_kernel(cur_ref, left_ref, right_ref, w_ref, b_ref, g_ref, beta_ref, o_ref, pad_ref, y_ref,
                   *, taps, tt, layer_norm):
    i = pl.program_id(1)
    n_i = pl.num_programs(1)
    nct = pad_ref.shape[0]
    lmask = (i > 0).astype(F32)
    rmask = (i < n_i - 1).astype(F32)
    for c in range(nct):
        sl = slice(c * LANES, (c + 1) * LANES)
        pad_ref[c, 0:HALO, :] = left_ref[0, :, sl].astype(F32) * lmask
        pad_ref[c, HALO:HALO + tt, :] = cur_ref[0, :, sl].astype(F32)
        pad_ref[c, HALO + tt:, :] = right_ref[0, :, sl].astype(F32) * rmask

    rows = 64
    first = HALO - taps // 2

    def tile_body(c, carry):
        accs = [jnp.broadcast_to(b_ref[c], (rows, LANES)) for _ in range(tt // rows)]
        for k in range(taps):
            wk = w_ref[c, k:k + 1, :]
            for rc in range(tt // rows):
                accs[rc] = accs[rc] + wk * pad_ref[c, pl.ds(first + k + rc * rows, rows), :]
        for rc in range(tt // rows):
            y_ref[c, rc * rows:(rc + 1) * rows, :] = accs[rc]
        return carry

    lax.fori_loop(0, nct, tile_body, 0)

    if layer_norm:
        tot = y_ref[0]
        for c in range(1, nct):
            tot = tot + y_ref[c]
        mu = jnp.sum(tot, axis=-1, keepdims=True) * (1.0 / (nct * LANES))
        sq = jnp.square(y_ref[0] - mu)
        for c in range(1, nct):
            sq = sq + jnp.square(y_ref[c] - mu)
        rstd = lax.rsqrt(jnp.sum(sq, axis=-1, keepdims=True) * (1.0 / (nct * LANES)) + NORM_EPS)
    for c in range(nct):
        sl = slice(c * LANES, (c + 1) * LANES)
        y = y_ref[c]
        if layer_norm:
            y = (y - mu) * rstd * g_ref[c] + beta_ref[c]
        o_ref[0, :, sl] = (y * _sigmoid(y)).astype(o_ref.dtype)


def _dwconv(x, w, b, g, beta, *, layer_norm, tt=256):
    bsz, t, c = x.shape
    taps = w.shape[0]
    nct = c // LANES
    w3 = w.reshape(taps, nct, LANES).transpose(1, 0, 2)
    tiles = lambda v: v.reshape(nct, 1, LANES)
    hb = tt // HALO
    n_halo = t // HALO
    small = lambda shape: pl.BlockSpec(shape, lambda bb, i: (0,) * len(shape))
    return pl.pallas_call(
        functools.partial(_dwconv_kernel, taps=taps, tt=tt, layer_norm=layer_norm),
        grid=(bsz, t // tt),
        in_specs=[pl.BlockSpec((1, tt, c), lambda bb, i: (bb, i, 0)),
                  pl.BlockSpec((1, HALO, c), lambda bb, i: (bb, jnp.maximum(i * hb - 1, 0), 0)),
                  pl.BlockSpec((1, HALO, c), lambda bb, i: (bb, jnp.minimum((i + 1) * hb, n_halo - 1), 0)),
                  small((nct, taps, LANES)), small((nct, 1, LANES)),
                  small((nct, 1, LANES)), small((nct, 1, LANES))],
        out_specs=pl.BlockSpec((1, tt, c), lambda bb, i: (bb, i, 0)),
        out_shape=jax.ShapeDtypeStruct((bsz, t, c), BF16),
        scratch_shapes=[pltpu.VMEM((nct, tt + 2 * HALO, LANES), F32), pltpu.VMEM((nct, tt, LANES), F32)],
        compiler_params=_params(("parallel", "parallel")),
        name="dwconv_ln" if layer_norm else "dwconv",
    )(x, x, x, w3, tiles(b), tiles(g), tiles(beta))


def _ssd_kernel(xs_ref, b_ref, c_ref, dt_ref, dtt_ref, alog_ref, alogc_ref, y_ref, state_ref, *, cpb):
    direction = pl.program_id(1)
    fwd = direction == 0
    q = SSD_CHUNK

    @pl.when(pl.program_id(2) == 0)
    def _():
        state_ref[...] = jnp.zeros_like(state_ref)

    row = lax.broadcasted_iota(jnp.int32, (q, q), 0)
    col = lax.broadcasted_iota(jnp.int32, (q, q), 1)
    sign = jnp.where(fwd, 1, -1)
    mask = (row - col) * sign >= 0
    tri = mask.astype(F32)
    tri_t = ((col - row) * sign >= 0).astype(F32)
    head_of_col = lax.broadcasted_iota(jnp.int32, (SSD_HEADS, D_MODEL), 1) // SSD_HEAD_DIM
    expand = (head_of_col == lax.broadcasted_iota(jnp.int32, (SSD_HEADS, D_MODEL), 0)).astype(F32)
    lane = lax.broadcasted_iota(jnp.int32, (q, LANES), 1)
    a_row = -jnp.exp(alog_ref[0])
    a_col = -jnp.exp(alogc_ref[0])
    hi = lax.Precision.HIGHEST

    for step in range(cpb):
        ci = jnp.where(fwd, step, cpb - 1 - step)
        r0 = pl.multiple_of(ci * q, q)
        x = xs_ref[0, pl.ds(r0, q), :]
        bm = b_ref[0, pl.ds(r0, q), :]
        cm = c_ref[0, pl.ds(r0, q), :]
        dt = dt_ref[0, 0, pl.ds(r0, q), :]
        dtt = dtt_ref[ci, 0]
        cs = jnp.dot(tri, dt * a_row, precision=hi, preferred_element_type=F32)
        cs_t = jnp.dot(dtt * a_col, tri_t, precision=hi, preferred_element_type=F32)
        tot = jnp.where(fwd, cs[q - 1:q, :], cs[0:1, :])
        scale = jnp.concatenate([dt * jnp.exp(tot - cs), jnp.exp(cs)], axis=0)
        scale_x = jnp.dot(scale, expand, precision=hi, preferred_element_type=F32)
        w_state = scale_x[:q]
        e_in = scale_x[q:]
        xw = (x.astype(F32) * w_state).astype(BF16)
        decay_row = jnp.where(fwd, e_in[q - 1:q, :], e_in[0:1, :])
        state = state_ref[...]
        state_bf = state.astype(BF16)
        new_states = []
        for g in range(SSD_GROUPS):
            gs = slice(g * SSD_STATE, (g + 1) * SSD_STATE)
            xg = slice(g * (D_MODEL // SSD_GROUPS), (g + 1) * (D_MODEL // SSD_GROUPS))
            bg, cg = bm[:, gs], cm[:, gs]
            cb = lax.dot_general(cg, bg, (((1,), (1,)), ((), ())), preferred_element_type=F32)
            y_off = jnp.dot(cg, state_bf[:, xg], preferred_element_type=F32) * e_in[:, xg]
            bg_t = bg.astype(F32).T.astype(BF16)
            new_states.append(jnp.dot(bg_t, xw[:, xg], preferred_element_type=F32))
            for pr in range(D_MODEL // SSD_GROUPS // LANES):
                t = g * (D_MODEL // SSD_GROUPS // LANES) + pr
                ms = []
                for h in (2 * t, 2 * t + 1):
                    seg = cs[:, h:h + 1] - cs_t[h:h + 1, :]
                    lmat = jnp.exp(jnp.where(mask, seg, -jnp.inf))
                    ms.append((cb * lmat * dtt[h:h + 1, :]).astype(BF16))
                m_cat = jnp.concatenate(ms, axis=1)
                xt = x[:, t * LANES:(t + 1) * LANES]
                zero = jnp.zeros_like(xt)
                x_bd = jnp.concatenate([jnp.where(lane < SSD_HEAD_DIM, xt, zero),
                                        jnp.where(lane >= SSD_HEAD_DIM, xt, zero)], axis=0)
                y_diag = jnp.dot(m_cat, x_bd, preferred_element_type=F32)
                y_ref[0, 0, pl.ds(r0, q), t * LANES:(t + 1) * LANES] = (
                    y_diag + y_off[:, pr * LANES:(pr + 1) * LANES]).astype(y_ref.dtype)
        state_ref[...] = state * decay_row + jnp.concatenate(new_states, axis=1)


def _ssd_scan(xbc_act, dt, dtt, a_log, rows=512):
    bsz, t, _ = xbc_act.shape
    nblk = t // rows
    cpb = rows // SSD_CHUNK
    blk = lambda d, i: jnp.where(d == 0, i, nblk - 1 - i)
    bc0 = D_MODEL // SSD_BC
    return pl.pallas_call(
        functools.partial(_ssd_kernel, cpb=cpb),
        grid=(bsz, 2, nblk),
        in_specs=[pl.BlockSpec((1, rows, D_MODEL), lambda b, d, i: (b, blk(d, i), 0)),
                  pl.BlockSpec((1, rows, SSD_BC), lambda b, d, i: (b, blk(d, i), bc0)),
                  pl.BlockSpec((1, rows, SSD_BC), lambda b, d, i: (b, blk(d, i), bc0 + 1)),
                  pl.BlockSpec((1, 1, rows, SSD_HEADS), lambda b, d, i: (d, b, blk(d, i), 0)),
                  pl.BlockSpec((cpb, 1, SSD_HEADS, SSD_CHUNK),
                               lambda b, d, i: (b * nblk + blk(d, i), d, 0, 0)),
                  pl.BlockSpec((1, 1, SSD_HEADS), lambda b, d, i: (d, 0, 0)),
                  pl.BlockSpec((1, SSD_HEADS, 1), lambda b, d, i: (d, 0, 0))],
        out_specs=pl.BlockSpec((1, 1, rows, D_MODEL), lambda b, d, i: (d, b, blk(d, i), 0)),
        out_shape=jax.ShapeDtypeStruct((2, bsz, t, D_MODEL), BF16),
        scratch_shapes=[pltpu.VMEM((SSD_STATE, D_MODEL), F32)],
        compiler_params=_params(("parallel", "parallel", "arbitrary")),
        name="ssd_scan",
    )(xbc_act, xbc_act, xbc_act, dt.reshape(2, bsz, t, SSD_HEADS), dtt,
      a_log.reshape(2, 1, SSD_HEADS), a_log.reshape(2, SSD_HEADS, 1))


def _attn_kernel(q_ref, k_ref, v_ref, slope_ref, o_ref, lse_ref, *, sub, dilation, radius, qb, kw):
    slope = slope_ref[0] * float(dilation)

    def body(qi, carry):
        q0 = pl.multiple_of(qi * qb, qb)
        k0 = pl.multiple_of(jnp.clip(q0 - radius, 0, sub - kw), radius)
        q = q_ref[0, pl.ds(q0, qb), :]
        k = k_ref[0, pl.ds(k0, kw), :]
        v = v_ref[0, pl.ds(k0, kw), :]
        s = lax.dot_general(q, k, (((1,), (1,)), ((), ())), preferred_element_type=F32)
        qpos = q0 + lax.broadcasted_iota(jnp.int32, (qb, kw), 0)
        kpos = k0 + lax.broadcasted_iota(jnp.int32, (qb, kw), 1)
        dist = jnp.abs(kpos - qpos)
        s = jnp.where(dist <= radius, s - slope[:, :1] * dist.astype(F32), NEG_INF)
        m = jnp.max(s, axis=-1, keepdims=True)
        p = jnp.exp(s - m)
        l = jnp.sum(p, axis=-1, keepdims=True)
        o = jnp.dot(p.astype(BF16), v, preferred_element_type=F32)
        o_ref[0, pl.ds(q0, qb), :] = (o / l).astype(o_ref.dtype)
        lse_ref[0, pl.ds(q0, qb), :] = jnp.broadcast_to(m + jnp.log(l), (qb, LANES))
        return carry

    lax.fori_loop(0, sub // qb, body, 0)


def _attention_group(qkv, slopes, group, window, dilation):
    bsz, t, width = qkv.shape
    sub = t // dilation
    radius = window // (2 * dilation)
    qb = min(2 * radius, sub)
    kw = min(qb + 2 * radius, sub)
    hpt = width // LANES
    view = qkv.reshape(bsz, sub, dilation * width)
    head0 = group * ATTN_HPG

    def spec(part):
        return pl.BlockSpec((1, sub, LANES),
                            lambda b, r, h: (b, 0, r * hpt + part * ATTN_HEADS + head0 + h))

    out_spec = pl.BlockSpec((1, sub, LANES), lambda b, r, h: (b, 0, r * ATTN_HPG + h))
    o, lse = pl.pallas_call(
        functools.partial(_attn_kernel, sub=sub, dilation=dilation, radius=radius, qb=qb, kw=kw),
        grid=(bsz, dilation, ATTN_HPG),
        in_specs=[spec(0), spec(1), spec(2),
                  pl.BlockSpec((1, 1, LANES), lambda b, r, h: (head0 + h, 0, 0))],
        out_specs=[out_spec, out_spec],
        out_shape=[jax.ShapeDtypeStruct((bsz, sub, dilation * ATTN_OUT), BF16),
                   jax.ShapeDtypeStruct((bsz, sub, dilation * ATTN_OUT), F32)],
        compiler_params=_params(("parallel", "parallel", "parallel")),
        name=f"attn_d{dilation}",
    )(view, view, view, slopes)
    return o.reshape(bsz * t, ATTN_OUT), lse.reshape(bsz * t, ATTN_OUT)


def _branch_kernel(yf_ref, yb_ref, xs_ref, sz_ref, d_ref, gs_ref,
                   o1_ref, o2_ref, o3_ref, l1_ref, l2_ref, l3_ref, yc_ref,
                   g1_ref, g2_ref, g3_ref, ws_ref, wa_ref, wc_ref, out_ref, ys_scr, ya_scr):
    @pl.when(pl.program_id(1) == 0)
    def _():
        y = yf_ref[...].astype(F32) + yb_ref[...].astype(F32) + d_ref[...] * xs_ref[...].astype(F32)
        y = y * sz_ref[...].astype(F32)
        ys_scr[...] = _rms(y, gs_ref[...]).astype(BF16)
        l1, l2, l3 = l1_ref[...], l2_ref[...], l3_ref[...]
        mx = jnp.maximum(jnp.maximum(l1, l2), l3)
        w1, w2, w3 = jnp.exp(l1 - mx), jnp.exp(l2 - mx), jnp.exp(l3 - mx)
        num = (w1 * o1_ref[...].astype(F32) + w2 * o2_ref[...].astype(F32) + w3 * o3_ref[...].astype(F32))
        ya_scr[...] = (num / (w1 + w2 + w3)).astype(BF16)

    acc = g1_ref[...].astype(F32) * jnp.dot(ys_scr[...], ws_ref[...], preferred_element_type=F32)
    acc += g2_ref[...].astype(F32) * jnp.dot(ya_scr[...], wa_ref[...], preferred_element_type=F32)
    acc += g3_ref[...].astype(F32) * jnp.dot(yc_ref[...], wc_ref[...], preferred_element_type=F32)
    out_ref[...] = acc.astype(out_ref.dtype)


def _branch_merge(y_dirs, xbc_act, sz, d_exp, g_ssd, att, yc, gates, w_s, w_a, w_c, tm=512, tn=512):
    m = sz.shape[0]
    d = D_MODEL
    nj = d // tn
    row = lambda w: pl.BlockSpec((tm, w), lambda i, j: (i, 0))
    vec = pl.BlockSpec((1, d), lambda i, j: (0, 0))
    gate = lambda g: pl.BlockSpec((tm, tn), lambda i, j: (i, g * nj + j))
    wspec = lambda k: pl.BlockSpec((k, tn), lambda i, j: (0, j))
    (o1, l1), (o2, l2), (o3, l3) = att
    return pl.pallas_call(
        _branch_kernel,
        grid=(m // tm, nj),
        in_specs=[row(d), row(d), row(d), row(d), vec, vec,
                  row(ATTN_OUT), row(ATTN_OUT), row(ATTN_OUT), row(ATTN_OUT), row(ATTN_OUT), row(ATTN_OUT),
                  row(d), gate(0), gate(1), gate(2), wspec(d), wspec(ATTN_OUT), wspec(d)],
        out_specs=pl.BlockSpec((tm, tn), lambda i, j: (i, j)),
        out_shape=jax.ShapeDtypeStruct((m, d), BF16),
        scratch_shapes=[pltpu.VMEM((tm, d), BF16), pltpu.VMEM((tm, ATTN_OUT), BF16)],
        compiler_params=_params(("parallel", "arbitrary")),
        name="branch_merge",
    )(y_dirs[0], y_dirs[1], xbc_act, sz, d_exp, g_ssd, o1, o2, o3, l1, l2, l3, yc,
      gates, gates, gates, w_s, w_a, w_c)


def _out_kernel(m_ref, w_ref, x_ref, g_ref, xo_ref, h_ref):
    xn = x_ref[...] + jnp.dot(m_ref[...], w_ref[...], preferred_element_type=F32)
    xo_ref[...] = xn
    h_ref[...] = _rms(xn, g_ref[...]).astype(h_ref.dtype)


def _out_proj(merged, w_out, x, g2, tm=256):
    m, d = x.shape
    row = pl.BlockSpec((tm, d), lambda i: (i, 0))
    return pl.pallas_call(
        _out_kernel,
        grid=(m // tm,),
        in_specs=[row, pl.BlockSpec((d, d), lambda i: (0, 0)), row, pl.BlockSpec((1, d), lambda i: (0, 0))],
        out_specs=[row, row],
        out_shape=[jax.ShapeDtypeStruct((m, d), F32), jax.ShapeDtypeStruct((m, d), BF16)],
        compiler_params=_params(("parallel",)),
        name="out_proj",
    )(merged, w_out, x, g2.reshape(1, d))


def _mlp_kernel(h_ref, wu_ref, wd_ref, x_ref, g_ref, xo_ref, hn_ref, acc_ref):
    j = pl.program_id(1)

    @pl.when(j == 0)
    def _():
        acc_ref[...] = jnp.zeros_like(acc_ref)

    u = jnp.maximum(jnp.dot(h_ref[...], wu_ref[...], preferred_element_type=F32), 0.0)
    acc_ref[...] += jnp.dot((u * u).astype(BF16), wd_ref[...], preferred_element_type=F32)

    @pl.when(j == pl.num_programs(1) - 1)
    def _():
        xn = x_ref[...] + acc_ref[...]
        xo_ref[...] = xn
        hn_ref[...] = _rms(xn, g_ref[...]).astype(hn_ref.dtype)


def _mlp(h2, w_up, w_down, x, g_next, tm=512, th=512):
    m, d = x.shape
    hid = w_up.shape[1]
    row = pl.BlockSpec((tm, d), lambda i, j: (i, 0))
    return pl.pallas_call(
        _mlp_kernel,
        grid=(m // tm, hid // th),
        in_specs=[row, pl.BlockSpec((d, th), lambda i, j: (0, j)), pl.BlockSpec((th, d), lambda i, j: (j, 0)),
                  row, pl.BlockSpec((1, d), lambda i, j: (0, 0))],
        out_specs=[row, row],
        out_shape=[jax.ShapeDtypeStruct((m, d), F32), jax.ShapeDtypeStruct((m, d), BF16)],
        scratch_shapes=[pltpu.VMEM((tm, d), F32)],
        compiler_params=_params(("parallel", "arbitrary")),
        name="mlp",
    )(h2, w_up, w_down, x, g_next.reshape(1, d))


def kernel(x, norm1_g, w_in, ssd_conv_w, ssd_conv_b, ssd_dt_bias, ssd_a_log, ssd_d, ssd_norm_g, w_ssd_o,
           q_norm_g, k_norm_g, w_attn_o, conv_dw_w, conv_dw_b, conv_ln_g, conv_ln_b, w_conv_o, w_out,
           norm2_g, w_mlp_up, w_mlp_down):
    bsz, t, d = x.shape
    m = bsz * t
    depth = w_in.shape[0]
    xf = x.reshape(m, d)
    slopes = jnp.exp2(-ALIBI_MAX_EXP * jnp.arange(1, ATTN_HEADS + 1, dtype=F32) / ATTN_HEADS)
    slopes = jnp.broadcast_to(slopes[:, None, None], (ATTN_HEADS, 1, LANES))
    ones_c = jnp.ones((SSD_XBC,), F32)
    zeros_c = jnp.zeros((SSD_XBC,), F32)
    h = _norm(xf, norm1_g[0])
    for i in range(depth):
        w = w_in[i].astype(BF16)
        sz = _mm(h, [w[:, COL_Z:COL_XBC]], [], _epi_silu, BF16, 1024, 512, "proj_z")
        xbc = _mm(h, [w[:, COL_XBC:COL_DT]], [], _epi_id, BF16, 1024, 512, "proj_xbc")
        dt, dtt = _dt_proj(h, w[:, COL_DT:COL_Q], ssd_dt_bias[i].reshape(-1))
        qk_gain = jnp.concatenate([jnp.tile(q_norm_g[i] * ATTN_HEAD_DIM ** -0.5, ATTN_HEADS),
                                   jnp.tile(k_norm_g[i], ATTN_HEADS),
                                   jnp.ones((ATTN_WIDTH,), F32)]).reshape(1, 3 * ATTN_WIDTH)
        qkv = _mm(h, [w[:, COL_Q:COL_GLU]], [qk_gain],
                  functools.partial(_epi_qkv, n_norm_tiles=2 * ATTN_WIDTH // 512), BF16, 1024, 512, "proj_qkv")
        glu = _mm(h, [w[:, COL_GLU:COL_GLU + D_MODEL], w[:, COL_GLU + D_MODEL:COL_GATE]], [],
                  _epi_glu, BF16, 1024, 512, "proj_glu")
        gates = _mm(h, [w[:, COL_GATE:]], [], _epi_sigmoid, BF16, 1024, 512, "proj_gates")

        xbc_act = _dwconv(xbc.reshape(bsz, t, SSD_XBC), ssd_conv_w[i], ssd_conv_b[i], ones_c, zeros_c,
                          layer_norm=False)
        y_dirs = _ssd_scan(xbc_act, dt, dtt, ssd_a_log[i])
        att = [_attention_group(qkv.reshape(bsz, t, 3 * ATTN_WIDTH), slopes, g, window, dilation)
               for g, (window, dilation) in enumerate(ATTN_PATTERNS)]
        yc = _dwconv(glu.reshape(bsz, t, D_MODEL), conv_dw_w[i], conv_dw_b[i], conv_ln_g[i], conv_ln_b[i],
                     layer_norm=True)

        merged = _branch_merge(
            y_dirs.reshape(2, m, d), xbc_act.reshape(m, SSD_XBC), sz,
            jnp.repeat(ssd_d[i], SSD_HEAD_DIM).reshape(1, d), ssd_norm_g[i].reshape(1, d),
            att, yc.reshape(m, d), gates,
            w_ssd_o[i].astype(BF16), w_attn_o[i].astype(BF16), w_conv_o[i].astype(BF16))
        xf, h2 = _out_proj(merged, w_out[i].astype(BF16), xf, norm2_g[i])
        g_next = norm1_g[i + 1] if i + 1 < depth else norm1_g[i]
        xf, h = _mlp(h2, w_mlp_up[i].astype(BF16), w_mlp_down[i].astype(BF16), xf, g_next)
    return xf.reshape(bsz, t, d)
```

```python
import functools
import math

import jax
import jax.numpy as jnp
from jax import lax
from jax.experimental import pallas as pl
from jax.experimental.pallas import tpu as pltpu

F32 = jnp.float32
BF16 = jnp.bfloat16

D_MODEL = 2048
SSD_HEAD_DIM = 64
SSD_HEADS = D_MODEL // SSD_HEAD_DIM
SSD_GROUPS = 4
SSD_STATE = 128
SSD_CHUNK = 128
SSD_BC = SSD_GROUPS * SSD_STATE
SSD_XBC = D_MODEL + 2 * SSD_BC
ATTN_HEAD_DIM = 128
ATTN_PATTERNS = ((128, 1), (512, 4), (2048, 16))
ATTN_HPG = 4
ATTN_HEADS = ATTN_HPG * len(ATTN_PATTERNS)
ATTN_WIDTH = ATTN_HEADS * ATTN_HEAD_DIM
ATTN_OUT = ATTN_HPG * ATTN_HEAD_DIM
ALIBI_MAX_EXP = 8.0
NEG_INF = -1e30
CONV_WIDTH = 31
MLP_HIDDEN = 4 * D_MODEL
NORM_EPS = 1e-6
N_BRANCHES = 3

COL_Z = 0
COL_XBC = COL_Z + D_MODEL
COL_DT = COL_XBC + SSD_XBC
COL_Q = COL_DT + 2 * SSD_HEADS
COL_K = COL_Q + ATTN_WIDTH
COL_V = COL_K + ATTN_WIDTH
COL_GLU = COL_V + ATTN_WIDTH
COL_GATE = COL_GLU + 2 * D_MODEL

P_Z = 0
P_GATE = P_Z + D_MODEL
P_GLU_A = P_GATE + N_BRANCHES * D_MODEL
P_GLU_G = P_GLU_A + D_MODEL
P_XBC = P_GLU_G + D_MODEL
P_QKV = P_XBC + SSD_XBC

LANES = 128
HALO = 16
VMEM_LIMIT = 56 * 1024 * 1024
LOG2E = math.log2(math.e)


def _params(sem):
    return pltpu.CompilerParams(dimension_semantics=sem, vmem_limit_bytes=VMEM_LIMIT)


def _rms(x, g):
    return x * lax.rsqrt(jnp.mean(x * x, axis=-1, keepdims=True) + NORM_EPS) * g


def _sigmoid(x):
    return 1.0 / (1.0 + jnp.exp(-x))


def _norm_kernel(x_ref, g_ref, o_ref):
    o_ref[...] = _rms(x_ref[...], g_ref[...]).astype(o_ref.dtype)


def _norm(x, g, tm=512):
    m, d = x.shape
    return pl.pallas_call(
        _norm_kernel,
        grid=(m // tm,),
        in_specs=[pl.BlockSpec((tm, d), lambda i: (i, 0)), pl.BlockSpec((1, d), lambda i: (0, 0))],
        out_specs=pl.BlockSpec((tm, d), lambda i: (i, 0)),
        out_shape=jax.ShapeDtypeStruct((m, d), BF16),
        compiler_params=_params(("parallel",)),
        name="rmsnorm",
    )(x, g.reshape(1, d))


CHUNK = 256


def _mm_kernel(*refs, nw, epilogue):
    a_ref = refs[0]
    w_refs = refs[1:1 + nw]
    o_ref = refs[-1]
    j = pl.program_id(1)
    for c in range(o_ref.shape[1] // CHUNK):
        cs = slice(c * CHUNK, (c + 1) * CHUNK)
        accs = [jnp.dot(a_ref[...], w[:, cs], preferred_element_type=F32) for w in w_refs]
        o_ref[:, cs] = epilogue(accs, j).astype(o_ref.dtype)


def _mm(a, w, col_starts, n, epilogue, tm, tn, name):
    m, k = a.shape
    in_specs = [pl.BlockSpec((tm, k), lambda i, j: (i, 0))]
    for s in col_starts:
        in_specs.append(pl.BlockSpec((k, tn), functools.partial(lambda i, j, off: (0, off + j), off=s // tn)))
    return pl.pallas_call(
        functools.partial(_mm_kernel, nw=len(col_starts), epilogue=epilogue),
        grid=(m // tm, n // tn),
        in_specs=in_specs,
        out_specs=pl.BlockSpec((tm, tn), lambda i, j: (i, j)),
        out_shape=jax.ShapeDtypeStruct((m, n), BF16),
        compiler_params=_params(("parallel", "arbitrary")),
        name=name,
    )(a, *([w] * len(col_starts)))


def _epi_id(accs, j):
    return accs[0]


def _epi_glu(accs, j):
    return accs[0] * _sigmoid(accs[1])


def _epi_z_gates(accs, j, *, n_silu_tiles):
    s = _sigmoid(accs[0])
    return jnp.where(j < n_silu_tiles, accs[0] * s, s)


def _qkv_kernel(a_ref, w_ref, gain_ref, o_ref, *scratch, dilation):
    j = pl.program_id(1)
    tm = a_ref.shape[0]
    per = tm // dilation
    for c in range(o_ref.shape[3] // CHUNK):
        cs = slice(c * CHUNK, (c + 1) * CHUNK)
        acc = jnp.dot(a_ref[...], w_ref[:, cs], preferred_element_type=F32)
        heads = []
        for hh in range(CHUNK // ATTN_HEAD_DIM):
            lo = hh * ATTN_HEAD_DIM
            heads.append(_rms(acc[:, lo:lo + ATTN_HEAD_DIM],
                              gain_ref[:, c * CHUNK + lo:c * CHUNK + lo + ATTN_HEAD_DIM]))
        val = jnp.where(j < 2, jnp.concatenate(heads, axis=1), acc)
        if dilation == 1:
            o_ref[0, 0, :, cs] = val.astype(o_ref.dtype)
        else:
            scr = scratch[c]
            for hh in range(CHUNK // LANES):
                scr[hh] = val[:, hh * LANES:(hh + 1) * LANES]
            for r in range(dilation):
                for hh in range(CHUNK // LANES):
                    lo = c * CHUNK + hh * LANES
                    o_ref[0, r, :, lo:lo + LANES] = scr[hh, pl.ds(r, per, stride=dilation), :].astype(o_ref.dtype)


def _qkv_proj(a, w, col_start, gains, bsz, t, dilation, tm=1024, tn=512):
    m, k = a.shape
    n = 3 * ATTN_OUT
    nt = t // tm
    off = col_start // tn
    scratch = [] if dilation == 1 else [pltpu.VMEM((CHUNK // LANES, tm, LANES), F32) for _ in range(tn // CHUNK)]
    return pl.pallas_call(
        functools.partial(_qkv_kernel, dilation=dilation),
        grid=(m // tm, n // tn),
        in_specs=[pl.BlockSpec((tm, k), lambda i, j: (i, 0)),
                  pl.BlockSpec((k, tn), lambda i, j: (0, off + j)),
                  pl.BlockSpec((1, tn), lambda i, j: (0, j))],
        out_specs=pl.BlockSpec((1, dilation, tm // dilation, tn), lambda i, j: (i // nt, 0, i % nt, j)),
        out_shape=jax.ShapeDtypeStruct((bsz, dilation, t // dilation, n), BF16),
        scratch_shapes=scratch,
        compiler_params=_params(("parallel", "arbitrary")),
        name=f"proj_qkv_d{dilation}",
    )(a, w, gains)


def _softplus(x):
    return jnp.maximum(x, 0.0) + jnp.log1p(jnp.exp(-jnp.abs(x)))


def _dt_kernel(h_ref, w_ref, wt_ref, b_ref, bt_ref, dt_ref, dtt_ref):
    h = h_ref[...]
    sp = _softplus(jnp.dot(h, w_ref[...], preferred_element_type=F32) + b_ref[...])
    dt_ref[0] = sp[:, :SSD_HEADS]
    dt_ref[1] = sp[:, SSD_HEADS:]
    sp_t = _softplus(lax.dot_general(wt_ref[...], h, (((1,), (1,)), ((), ())),
                                     preferred_element_type=F32) + bt_ref[...])
    for c in range(sp_t.shape[1] // SSD_CHUNK):
        blk = sp_t[:, c * SSD_CHUNK:(c + 1) * SSD_CHUNK]
        dtt_ref[c, 0] = blk[:SSD_HEADS]
        dtt_ref[c, 1] = blk[SSD_HEADS:]


def _dt_proj(h, w_dt, bias, tm=512):
    m, k = h.shape
    nd = 2 * SSD_HEADS
    cpb = tm // SSD_CHUNK
    return pl.pallas_call(
        _dt_kernel,
        grid=(m // tm,),
        in_specs=[pl.BlockSpec((tm, k), lambda i: (i, 0)),
                  pl.BlockSpec((k, nd), lambda i: (0, 0)),
                  pl.BlockSpec((nd, k), lambda i: (0, 0)),
                  pl.BlockSpec((1, nd), lambda i: (0, 0)),
                  pl.BlockSpec((nd, 1), lambda i: (0, 0))],
        out_specs=[pl.BlockSpec((2, tm, SSD_HEADS), lambda i: (0, i, 0)),
                   pl.BlockSpec((cpb, 2, SSD_HEADS, SSD_CHUNK), lambda i: (i, 0, 0, 0))],
        out_shape=[jax.ShapeDtypeStruct((2, m, SSD_HEADS), F32),
                   jax.ShapeDtypeStruct((m // SSD_CHUNK, 2, SSD_HEADS, SSD_CHUNK), F32)],
        compiler_params=_params(("parallel",)),
        name="dt_proj",
    )(h, w_dt, w_dt.T, bias.reshape(1, nd), bias.reshape(nd, 1))


def _dwconv_kernel(cur_ref, left_ref, right_ref, w_ref, b_ref, g_ref, beta_ref, o_ref, pad_ref, y_ref,
                   *, taps, tt, layer_norm):
    i = pl.program_id(1)
    n_i = pl.num_programs(1)
    nct = pad_ref.shape[0]
    lmask = (i > 0).astype(F32)
    rmask = (i < n_i - 1).astype(F32)
    for c in range(nct):
        sl = slice(c * LANES, (c + 1) * LANES)
        pad_ref[c, 0:HALO, :] = left_ref[0, :, sl].astype(F32) * lmask
        pad_ref[c, HALO:HALO + tt, :] = cur_ref[0, :, sl].astype(F32)
        pad_ref[c, HALO + tt:, :] = right_ref[0, :, sl].astype(F32) * rmask

    rows = 64
    first = HALO - taps // 2

    def tile_body(c, carry):
        accs = [jnp.broadcast_to(b_ref[c], (rows, LANES)) for _ in range(tt // rows)]
        for k in range(taps):
            wk = w_ref[c, k:k + 1, :]
            for rc in range(tt // rows):
                accs[rc] = accs[rc] + wk * pad_ref[c, pl.ds(first + k + rc * rows, rows), :]
        for rc in range(tt // rows):
            y_ref[c, rc * rows:(rc + 1) * rows, :] = accs[rc]
        return carry

    lax.fori_loop(0, nct, tile_body, 0)

    if layer_norm:
        tot = y_ref[0]
        for c in range(1, nct):
            tot = tot + y_ref[c]
        mu = jnp.sum(tot, axis=-1, keepdims=True) * (1.0 / (nct * LANES))
        sq = jnp.square(y_ref[0] - mu)
        for c in range(1, nct):
            sq = sq + jnp.square(y_ref[c] - mu)
        rstd = lax.rsqrt(jnp.sum(sq, axis=-1, keepdims=True) * (1.0 / (nct * LANES)) + NORM_EPS)
    for c in range(nct):
        sl = slice(c * LANES, (c + 1) * LANES)
        y = y_ref[c]
        if layer_norm:
            y = (y - mu) * rstd * g_ref[c] + beta_ref[c]
        o_ref[0, :, sl] = (y * _sigmoid(y)).astype(o_ref.dtype)


def _dwconv(x, w, b, g, beta, *, layer_norm, tt=256):
    bsz, t, c = x.shape
    taps = w.shape[0]
    nct = c // LANES
    w3 = w.reshape(taps, nct, LANES).transpose(1, 0, 2)
    tiles = lambda v: v.reshape(nct, 1, LANES)
    hb = tt // HALO
    n_halo = t // HALO
    small = lambda shape: pl.BlockSpec(shape, lambda bb, i: (0,) * len(shape))
    return pl.pallas_call(
        functools.partial(_dwconv_kernel, taps=taps, tt=tt, layer_norm=layer_norm),
        grid=(bsz, t // tt),
        in_specs=[pl.BlockSpec((1, tt, c), lambda bb, i: (bb, i, 0)),
                  pl.BlockSpec((1, HALO, c), lambda bb, i: (bb, jnp.maximum(i * hb - 1, 0), 0)),
                  pl.BlockSpec((1, HALO, c), lambda bb, i: (bb, jnp.minimum((i + 1) * hb, n_halo - 1), 0)),
                  small((nct, taps, LANES)), small((nct, 1, LANES)),
                  small((nct, 1, LANES)), small((nct, 1, LANES))],
        out_specs=pl.BlockSpec((1, tt, c), lambda bb, i: (bb, i, 0)),
        out_shape=jax.ShapeDtypeStruct((bsz, t, c), BF16),
        scratch_shapes=[pltpu.VMEM((nct, tt + 2 * HALO, LANES), F32), pltpu.VMEM((nct, tt, LANES), F32)],
        compiler_params=_params(("parallel", "parallel")),
        name="dwconv_ln" if layer_norm else "dwconv",
    )(x, x, x, w3, tiles(b), tiles(g), tiles(beta))


def _split3(v):
    hi = v.astype(BF16)
    r1 = v - hi.astype(F32)
    mid = r1.astype(BF16)
    lo = (r1 - mid.astype(F32)).astype(BF16)
    return jnp.concatenate([hi, mid, lo], axis=1)


def _ssd_kernel(xs_ref, b_ref, c_ref, dt_ref, dtt_ref, alog_ref, alogc_ref, y_ref, state_ref, *, cpb):
    direction = pl.program_id(1)
    fwd = direction == 0
    q = SSD_CHUNK

    @pl.when(pl.program_id(2) == 0)
    def _():
        state_ref[...] = jnp.zeros_like(state_ref)

    row = lax.broadcasted_iota(jnp.int32, (q, q), 0)
    col = lax.broadcasted_iota(jnp.int32, (q, q), 1)
    sign = jnp.where(fwd, 1, -1)
    mask = (row - col) * sign >= 0
    tri = mask.astype(F32)
    tri_t = ((col - row) * sign >= 0).astype(F32)
    head_of_col = lax.broadcasted_iota(jnp.int32, (3 * SSD_HEADS, D_MODEL), 1) // SSD_HEAD_DIM
    head_of_row = lax.broadcasted_iota(jnp.int32, (3 * SSD_HEADS, D_MODEL), 0) % SSD_HEADS
    expand3 = jnp.where(head_of_col == head_of_row, 1.0, 0.0).astype(BF16)
    lane = lax.broadcasted_iota(jnp.int32, (q, LANES), 1)
    a_row = -jnp.exp(alog_ref[0])
    a_col = -jnp.exp(alogc_ref[0])
    hi = lax.Precision.HIGHEST

    for step in range(cpb):
        ci = jnp.where(fwd, step, cpb - 1 - step)
        r0 = pl.multiple_of(ci * q, q)
        x = xs_ref[0, pl.ds(r0, q), :]
        bm = b_ref[0, pl.ds(r0, q), :]
        cm = c_ref[0, pl.ds(r0, q), :]
        dt = dt_ref[0, 0, pl.ds(r0, q), :]
        dtt = dtt_ref[ci, 0]
        cs = jnp.dot(tri, dt * a_row, precision=hi, preferred_element_type=F32)
        cs_t = jnp.dot(dtt * a_col, tri_t, precision=hi, preferred_element_type=F32)
        tot = jnp.where(fwd, cs[q - 1:q, :], cs[0:1, :])
        scale = jnp.concatenate([dt * jnp.exp(tot - cs), jnp.exp(cs)], axis=0)
        scale_x = jnp.dot(_split3(scale), expand3, preferred_element_type=F32)
        w_state = scale_x[:q]
        e_in = scale_x[q:]
        xw = (x.astype(F32) * w_state).astype(BF16)
        decay_row = jnp.where(fwd, e_in[q - 1:q, :], e_in[0:1, :])
        cs2 = cs * LOG2E
        cs2_t = cs_t * LOG2E - jnp.log(dtt) * LOG2E
        state = state_ref[...]
        state_bf = state.astype(BF16)
        new_states = []
        for g in range(SSD_GROUPS):
            gs = slice(g * SSD_STATE, (g + 1) * SSD_STATE)
            xg = slice(g * (D_MODEL // SSD_GROUPS), (g + 1) * (D_MODEL // SSD_GROUPS))
            bg, cg = bm[:, gs], cm[:, gs]
            cb = lax.dot_general(cg, bg, (((1,), (1,)), ((), ())), preferred_element_type=F32)
            y_off = jnp.dot(cg, state_bf[:, xg], preferred_element_type=F32) * e_in[:, xg]
            bg_t = bg.astype(F32).T.astype(BF16)
            new_states.append(jnp.dot(bg_t, xw[:, xg], preferred_element_type=F32))
            for pr in range(D_MODEL // SSD_GROUPS // LANES):
                t = g * (D_MODEL // SSD_GROUPS // LANES) + pr
                ms = []
                for h in (2 * t, 2 * t + 1):
                    seg = cs2[:, h:h + 1] - cs2_t[h:h + 1, :]
                    ms.append((cb * jnp.exp2(jnp.where(mask, seg, -jnp.inf))).astype(BF16))
                m_cat = jnp.concatenate(ms, axis=1)
                xt = x[:, t * LANES:(t + 1) * LANES]
                zero = jnp.zeros_like(xt)
                x_bd = jnp.concatenate([jnp.where(lane < SSD_HEAD_DIM, xt, zero),
                                        jnp.where(lane >= SSD_HEAD_DIM, xt, zero)], axis=0)
                y_diag = jnp.dot(m_cat, x_bd, preferred_element_type=F32)
                y_ref[0, 0, pl.ds(r0, q), t * LANES:(t + 1) * LANES] = (
                    y_diag + y_off[:, pr * LANES:(pr + 1) * LANES]).astype(y_ref.dtype)
        state_ref[...] = state * decay_row + jnp.concatenate(new_states, axis=1)


def _ssd_scan(xbc_act, dt, dtt, a_log, rows=512):
    bsz, t, _ = xbc_act.shape
    nblk = t // rows
    cpb = rows // SSD_CHUNK
    blk = lambda d, i: jnp.where(d == 0, i, nblk - 1 - i)
    bc0 = D_MODEL // SSD_BC
    return pl.pallas_call(
        functools.partial(_ssd_kernel, cpb=cpb),
        grid=(bsz, 2, nblk),
        in_specs=[pl.BlockSpec((1, rows, D_MODEL), lambda b, d, i: (b, blk(d, i), 0)),
                  pl.BlockSpec((1, rows, SSD_BC), lambda b, d, i: (b, blk(d, i), bc0)),
                  pl.BlockSpec((1, rows, SSD_BC), lambda b, d, i: (b, blk(d, i), bc0 + 1)),
                  pl.BlockSpec((1, 1, rows, SSD_HEADS), lambda b, d, i: (d, b, blk(d, i), 0)),
                  pl.BlockSpec((cpb, 1, SSD_HEADS, SSD_CHUNK),
                               lambda b, d, i: (b * nblk + blk(d, i), d, 0, 0)),
                  pl.BlockSpec((1, 1, SSD_HEADS), lambda b, d, i: (d, 0, 0)),
                  pl.BlockSpec((1, SSD_HEADS, 1), lambda b, d, i: (d, 0, 0))],
        out_specs=pl.BlockSpec((1, 1, rows, D_MODEL), lambda b, d, i: (d, b, blk(d, i), 0)),
        out_shape=jax.ShapeDtypeStruct((2, bsz, t, D_MODEL), BF16),
        scratch_shapes=[pltpu.VMEM((SSD_STATE, D_MODEL), F32)],
        compiler_params=_params(("parallel", "parallel", "arbitrary")),
        name="ssd_scan",
    )(xbc_act, xbc_act, xbc_act, dt.reshape(2, bsz, t, SSD_HEADS), dtt,
      a_log.reshape(2, 1, SSD_HEADS), a_log.reshape(2, SSD_HEADS, 1))


def _attn_kernel(q_ref, k_ref, v_ref, slope_ref, o_ref, lse_ref, *, sub, dilation, radius, qb, kw):
    def body(qi, carry):
        q0 = pl.multiple_of(qi * qb, qb)
        k0 = pl.multiple_of(jnp.clip(q0 - radius, 0, sub - kw), radius)
        qpos = q0 + lax.broadcasted_iota(jnp.int32, (qb, kw), 0)
        kpos = k0 + lax.broadcasted_iota(jnp.int32, (qb, kw), 1)
        dist = jnp.abs(kpos - qpos)
        valid = dist <= radius
        neg_dist = dist.astype(F32) * (-float(dilation))
        for h in range(ATTN_HPG):
            hs = slice(h * ATTN_HEAD_DIM, (h + 1) * ATTN_HEAD_DIM)
            q = q_ref[0, 0, pl.ds(q0, qb), hs]
            k = k_ref[0, 0, pl.ds(k0, kw), hs]
            v = v_ref[0, 0, pl.ds(k0, kw), hs]
            s = lax.dot_general(q, k, (((1,), (1,)), ((), ())), preferred_element_type=F32)
            s = jnp.where(valid, s + slope_ref[0, h:h + 1, :1] * neg_dist, NEG_INF)
            m = jnp.max(s, axis=-1, keepdims=True)
            p = jnp.exp(s - m)
            l = jnp.sum(p, axis=-1, keepdims=True)
            o = jnp.dot(p.astype(BF16), v, preferred_element_type=F32)
            o_ref[0, 0, pl.ds(q0, qb), hs] = (o / l).astype(o_ref.dtype)
            lse_ref[0, 0, pl.ds(q0, qb), hs] = jnp.broadcast_to(m + jnp.log(l), (qb, LANES))
        return carry

    lax.fori_loop(0, sub // qb, body, 0)


def _attention_group(qkv, slopes, group, window, dilation):
    bsz, _, sub, _ = qkv.shape
    radius = window // (2 * dilation)
    qb = min(2 * radius, sub)
    kw = min(qb + 2 * radius, sub)
    spec = lambda part: pl.BlockSpec((1, 1, sub, ATTN_OUT), lambda b, r: (b, r, 0, part))
    out_spec = pl.BlockSpec((1, 1, sub, ATTN_OUT), lambda b, r: (b, r, 0, 0))
    return pl.pallas_call(
        functools.partial(_attn_kernel, sub=sub, dilation=dilation, radius=radius, qb=qb, kw=kw),
        grid=(bsz, dilation),
        in_specs=[spec(0), spec(1), spec(2),
                  pl.BlockSpec((1, ATTN_HPG, LANES), lambda b, r: (group, 0, 0))],
        out_specs=[out_spec, out_spec],
        out_shape=[jax.ShapeDtypeStruct((bsz, dilation, sub, ATTN_OUT), BF16),
                   jax.ShapeDtypeStruct((bsz, dilation, sub, ATTN_OUT), F32)],
        compiler_params=_params(("parallel", "parallel")),
        name=f"attn_d{dilation}",
    )(qkv, qkv, qkv, slopes)


def _branch_kernel(yf_ref, yb_ref, xs_ref, sz_ref, d_ref, gs_ref,
                   o1_ref, o2_ref, o3_ref, l1_ref, l2_ref, l3_ref, yc_ref,
                   g1_ref, g2_ref, g3_ref, ws_ref, wa_ref, wc_ref, out_ref,
                   ys_scr, ya_scr, o2_scr, o3_scr, l2_scr, l3_scr):
    @pl.when(pl.program_id(1) == 0)
    def _():
        y = yf_ref[...].astype(F32) + yb_ref[...].astype(F32) + d_ref[...] * xs_ref[...].astype(F32)
        y = y * sz_ref[...].astype(F32)
        ys_scr[...] = _rms(y, gs_ref[...]).astype(BF16)
        for src_o, src_l, dst_o, dst_l in ((o2_ref, l2_ref, o2_scr, l2_scr), (o3_ref, l3_ref, o3_scr, l3_scr)):
            dil, per = src_o.shape[1], src_o.shape[2]
            for r in range(dil):
                for hh in range(ATTN_HPG):
                    hs = slice(hh * LANES, (hh + 1) * LANES)
                    dst_o[hh, pl.ds(r, per, stride=dil), :] = src_o[0, r, :, hs].astype(F32)
                    dst_l[hh, pl.ds(r, per, stride=dil), :] = src_l[0, r, :, hs]
        for hh in range(ATTN_HPG):
            hs = slice(hh * LANES, (hh + 1) * LANES)
            l1, l2, l3 = l1_ref[0, 0, :, hs], l2_scr[hh], l3_scr[hh]
            mx = jnp.maximum(jnp.maximum(l1, l2), l3)
            w1, w2, w3 = jnp.exp(l1 - mx), jnp.exp(l2 - mx), jnp.exp(l3 - mx)
            num = w1 * o1_ref[0, 0, :, hs].astype(F32) + w2 * o2_scr[hh] + w3 * o3_scr[hh]
            ya_scr[:, hs] = (num / (w1 + w2 + w3)).astype(BF16)

    for c in range(out_ref.shape[1] // CHUNK):
        cs = slice(c * CHUNK, (c + 1) * CHUNK)
        acc = g1_ref[:, cs].astype(F32) * jnp.dot(ys_scr[...], ws_ref[:, cs], preferred_element_type=F32)
        acc += g2_ref[:, cs].astype(F32) * jnp.dot(ya_scr[...], wa_ref[:, cs], preferred_element_type=F32)
        acc += g3_ref[:, cs].astype(F32) * jnp.dot(yc_ref[...], wc_ref[:, cs], preferred_element_type=F32)
        out_ref[:, cs] = acc.astype(out_ref.dtype)


def _branch_merge(y_dirs, xbc_act, zg, d_exp, g_ssd, att, yc, w_s, w_a, w_c, t, tm=512, tn=512):
    m = zg.shape[0]
    d = D_MODEL
    nj = d // tn
    nt = t // tm
    row = lambda w: pl.BlockSpec((tm, w), lambda i, j: (i, 0))
    vec = pl.BlockSpec((1, d), lambda i, j: (0, 0))
    gate = lambda g: pl.BlockSpec((tm, tn), lambda i, j: (i, (P_GATE + g * d) // tn + j))
    wspec = lambda k: pl.BlockSpec((k, tn), lambda i, j: (0, j))
    att_spec = lambda dil: pl.BlockSpec((1, dil, tm // dil, ATTN_OUT), lambda i, j: (i // nt, 0, i % nt, 0))
    (o1, l1), (o2, l2), (o3, l3) = att
    dils = [dil for _, dil in ATTN_PATTERNS]
    return pl.pallas_call(
        _branch_kernel,
        grid=(m // tm, nj),
        in_specs=[row(d), row(d), row(d), row(d), vec, vec,
                  att_spec(dils[0]), att_spec(dils[1]), att_spec(dils[2]),
                  att_spec(dils[0]), att_spec(dils[1]), att_spec(dils[2]),
                  row(d), gate(0), gate(1), gate(2), wspec(d), wspec(ATTN_OUT), wspec(d)],
        out_specs=pl.BlockSpec((tm, tn), lambda i, j: (i, j)),
        out_shape=jax.ShapeDtypeStruct((m, d), BF16),
        scratch_shapes=[pltpu.VMEM((tm, d), BF16), pltpu.VMEM((tm, ATTN_OUT), BF16)]
        + [pltpu.VMEM((ATTN_HPG, tm, LANES), F32) for _ in range(4)],
        compiler_params=_params(("parallel", "arbitrary")),
        name="branch_merge",
    )(y_dirs[0], y_dirs[1], xbc_act, zg, d_exp, g_ssd, o1, o2, o3, l1, l2, l3, yc,
      zg, zg, zg, w_s, w_a, w_c)


def _out_kernel(m_ref, w_ref, x_ref, g_ref, xo_ref, h_ref):
    for c in range(xo_ref.shape[1] // CHUNK):
        cs = slice(c * CHUNK, (c + 1) * CHUNK)
        xo_ref[:, cs] = x_ref[:, cs] + jnp.dot(m_ref[...], w_ref[:, cs], preferred_element_type=F32)
    h_ref[...] = _rms(xo_ref[...], g_ref[...]).astype(h_ref.dtype)


def _out_proj(merged, w_out, x, g2, tm=256):
    m, d = x.shape
    row = pl.BlockSpec((tm, d), lambda i: (i, 0))
    return pl.pallas_call(
        _out_kernel,
        grid=(m // tm,),
        in_specs=[row, pl.BlockSpec((d, d), lambda i: (0, 0)), row, pl.BlockSpec((1, d), lambda i: (0, 0))],
        out_specs=[row, row],
        out_shape=[jax.ShapeDtypeStruct((m, d), F32), jax.ShapeDtypeStruct((m, d), BF16)],
        compiler_params=_params(("parallel",)),
        name="out_proj",
    )(merged, w_out, x, g2.reshape(1, d))


def _mlp_kernel(h_ref, wu_ref, wd_ref, x_ref, g_ref, xo_ref, hn_ref, acc_ref):
    j = pl.program_id(1)

    @pl.when(j == 0)
    def _():
        acc_ref[...] = x_ref[...]

    us = []
    for c in range(wu_ref.shape[1] // CHUNK):
        u = jnp.maximum(jnp.dot(h_ref[...], wu_ref[:, c * CHUNK:(c + 1) * CHUNK], preferred_element_type=F32), 0.0)
        us.append((u * u).astype(BF16))
    u = jnp.concatenate(us, axis=1)
    for c in range(acc_ref.shape[1] // CHUNK):
        cs = slice(c * CHUNK, (c + 1) * CHUNK)
        acc_ref[:, cs] += jnp.dot(u, wd_ref[:, cs], preferred_element_type=F32)

    @pl.when(j == pl.num_programs(1) - 1)
    def _():
        xn = acc_ref[...]
        xo_ref[...] = xn
        hn_ref[...] = _rms(xn, g_ref[...]).astype(hn_ref.dtype)


def _mlp(h2, w_up, w_down, x, g_next, tm=512, th=1024):
    m, d = x.shape
    hid = w_up.shape[1]
    row = pl.BlockSpec((tm, d), lambda i, j: (i, 0))
    return pl.pallas_call(
        _mlp_kernel,
        grid=(m // tm, hid // th),
        in_specs=[row, pl.BlockSpec((d, th), lambda i, j: (0, j)), pl.BlockSpec((th, d), lambda i, j: (j, 0)),
                  row, pl.BlockSpec((1, d), lambda i, j: (0, 0))],
        out_specs=[row, row],
        out_shape=[jax.ShapeDtypeStruct((m, d), F32), jax.ShapeDtypeStruct((m, d), BF16)],
        scratch_shapes=[pltpu.VMEM((tm, d), F32)],
        compiler_params=_params(("parallel", "arbitrary")),
        name="mlp",
    )(h2, w_up, w_down, x, g_next.reshape(1, d))


def _pack_in_proj(w):
    hw = ATTN_OUT
    qkv = [w[:, base + g * hw:base + (g + 1) * hw] for g in range(len(ATTN_PATTERNS))
           for base in (COL_Q, COL_K, COL_V)]
    packed = jnp.concatenate([w[:, COL_Z:COL_XBC], w[:, COL_GATE:], w[:, COL_GLU:COL_GATE],
                              w[:, COL_XBC:COL_DT]] + qkv, axis=1)
    return packed.astype(BF16), w[:, COL_DT:COL_Q].astype(BF16)


def kernel(x, norm1_g, w_in, ssd_conv_w, ssd_conv_b, ssd_dt_bias, ssd_a_log, ssd_d, ssd_norm_g, w_ssd_o,
           q_norm_g, k_norm_g, w_attn_o, conv_dw_w, conv_dw_b, conv_ln_g, conv_ln_b, w_conv_o, w_out,
           norm2_g, w_mlp_up, w_mlp_down):
    bsz, t, d = x.shape
    m = bsz * t
    depth = w_in.shape[0]
    xf = x.reshape(m, d)
    slopes = jnp.exp2(-ALIBI_MAX_EXP * jnp.arange(1, ATTN_HEADS + 1, dtype=F32) / ATTN_HEADS)
    slopes = jnp.broadcast_to(slopes.reshape(len(ATTN_PATTERNS), ATTN_HPG, 1),
                              (len(ATTN_PATTERNS), ATTN_HPG, LANES))
    ones_c = jnp.ones((SSD_XBC,), F32)
    zeros_c = jnp.zeros((SSD_XBC,), F32)
    h = _norm(xf, norm1_g[0])
    for i in range(depth):
        w, w_dt = _pack_in_proj(w_in[i])
        zg = _mm(h, w, [P_Z], P_GLU_A, functools.partial(_epi_z_gates, n_silu_tiles=D_MODEL // 1024),
                 1024, 1024, "proj_z_gates")
        glu = _mm(h, w, [P_GLU_A, P_GLU_G], D_MODEL, _epi_glu, 1024, 1024, "proj_glu")
        xbc = _mm(h, w, [P_XBC], SSD_XBC, _epi_id, 1024, 1024, "proj_xbc")
        dt, dtt = _dt_proj(h, w_dt, ssd_dt_bias[i].reshape(-1))
        qk_gain = jnp.concatenate([jnp.tile(q_norm_g[i] * ATTN_HEAD_DIM ** -0.5, ATTN_HPG),
                                   jnp.tile(k_norm_g[i], ATTN_HPG),
                                   jnp.ones((ATTN_OUT,), F32)]).reshape(1, 3 * ATTN_OUT)
        att = []
        for g, (window, dilation) in enumerate(ATTN_PATTERNS):
            qkv = _qkv_proj(h, w, P_QKV + g * 3 * ATTN_OUT, qk_gain, bsz, t, dilation)
            att.append(_attention_group(qkv, slopes, g, window, dilation))

        xbc_act = _dwconv(xbc.reshape(bsz, t, SSD_XBC), ssd_conv_w[i], ssd_conv_b[i], ones_c, zeros_c,
                          layer_norm=False)
        y_dirs = _ssd_scan(xbc_act, dt, dtt, ssd_a_log[i])
        yc = _dwconv(glu.reshape(bsz, t, D_MODEL), conv_dw_w[i], conv_dw_b[i], conv_ln_g[i], conv_ln_b[i],
                     layer_norm=True)

        merged = _branch_merge(
            y_dirs.reshape(2, m, d), xbc_act.reshape(m, SSD_XBC), zg,
            jnp.repeat(ssd_d[i], SSD_HEAD_DIM).reshape(1, d), ssd_norm_g[i].reshape(1, d),
            att, yc.reshape(m, d),
            w_ssd_o[i].astype(BF16), w_attn_o[i].astype(BF16), w_conv_o[i].astype(BF16), t)
        xf, h2 = _out_proj(merged, w_out[i].astype(BF16), xf, norm2_g[i])
        g_next = norm1_g[i + 1] if i + 1 < depth else norm1_g[i]
        xf, h = _mlp(h2, w_mlp_up[i].astype(BF16), w_mlp_down[i].astype(BF16), xf, g_next)
    return xf.reshape(bsz, t, d)
```

```python
import functools
import math

import jax
import jax.numpy as jnp
from jax import lax
from jax.experimental import pallas as pl
from jax.experimental.pallas import tpu as pltpu

F32 = jnp.float32
BF16 = jnp.bfloat16

D_MODEL = 2048
SSD_HEAD_DIM = 64
SSD_HEADS = D_MODEL // SSD_HEAD_DIM
SSD_GROUPS = 4
SSD_STATE = 128
SSD_CHUNK = 128
SSD_BC = SSD_GROUPS * SSD_STATE
SSD_XBC = D_MODEL + 2 * SSD_BC
GROUP_W = D_MODEL // SSD_GROUPS
ATTN_HEAD_DIM = 128
ATTN_PATTERNS = ((128, 1), (512, 4), (2048, 16))
ATTN_HPG = 4
ATTN_HEADS = ATTN_HPG * len(ATTN_PATTERNS)
ATTN_WIDTH = ATTN_HEADS * ATTN_HEAD_DIM
ATTN_OUT = ATTN_HPG * ATTN_HEAD_DIM
ALIBI_MAX_EXP = 8.0
NEG_INF = -1e30
CONV_WIDTH = 31
MLP_HIDDEN = 4 * D_MODEL
NORM_EPS = 1e-6
N_BRANCHES = 3

COL_Z = 0
COL_XBC = COL_Z + D_MODEL
COL_DT = COL_XBC + SSD_XBC
COL_Q = COL_DT + 2 * SSD_HEADS
COL_K = COL_Q + ATTN_WIDTH
COL_V = COL_K + ATTN_WIDTH
COL_GLU = COL_V + ATTN_WIDTH
COL_GATE = COL_GLU + 2 * D_MODEL

P_Z = 0
P_GATE = P_Z + D_MODEL
P_GLU_A = P_GATE + N_BRANCHES * D_MODEL
P_GLU_G = P_GLU_A + D_MODEL
P_XBC = P_GLU_G + D_MODEL
P_QKV = P_XBC + SSD_XBC

LANES = 128
HALO = 16
VMEM_LIMIT = 56 * 1024 * 1024
LOG2E = math.log2(math.e)


def _params(sem):
    return pltpu.CompilerParams(dimension_semantics=sem, vmem_limit_bytes=VMEM_LIMIT)


def _rms(x, g):
    return x * lax.rsqrt(jnp.mean(x * x, axis=-1, keepdims=True) + NORM_EPS) * g


def _sigmoid(x):
    return 0.5 * jnp.tanh(0.5 * x) + 0.5


def _norm_kernel(x_ref, g_ref, o_ref):
    o_ref[...] = _rms(x_ref[...], g_ref[...]).astype(o_ref.dtype)


def _norm(x, g, tm=512):
    m, d = x.shape
    return pl.pallas_call(
        _norm_kernel,
        grid=(m // tm,),
        in_specs=[pl.BlockSpec((tm, d), lambda i: (i, 0)), pl.BlockSpec((1, d), lambda i: (0, 0))],
        out_specs=pl.BlockSpec((tm, d), lambda i: (i, 0)),
        out_shape=jax.ShapeDtypeStruct((m, d), BF16),
        compiler_params=_params(("parallel",)),
        name="rmsnorm",
    )(x, g.reshape(1, d))


CHUNK = 256


def _mm_kernel(*refs, nw, epilogue):
    a_ref = refs[0]
    w_refs = refs[1:1 + nw]
    o_ref = refs[-1]
    j = pl.program_id(1)
    for c in range(o_ref.shape[1] // CHUNK):
        cs = slice(c * CHUNK, (c + 1) * CHUNK)
        accs = [jnp.dot(a_ref[...], w[:, cs], preferred_element_type=F32) for w in w_refs]
        o_ref[:, cs] = epilogue(accs, j).astype(o_ref.dtype)


def _mm(a, w, col_starts, n, epilogue, tm, tn, name):
    m, k = a.shape
    in_specs = [pl.BlockSpec((tm, k), lambda i, j: (i, 0))]
    for s in col_starts:
        in_specs.append(pl.BlockSpec((k, tn), functools.partial(lambda i, j, off: (0, off + j), off=s // tn)))
    return pl.pallas_call(
        functools.partial(_mm_kernel, nw=len(col_starts), epilogue=epilogue),
        grid=(m // tm, n // tn),
        in_specs=in_specs,
        out_specs=pl.BlockSpec((tm, tn), lambda i, j: (i, j)),
        out_shape=jax.ShapeDtypeStruct((m, n), BF16),
        compiler_params=_params(("parallel", "arbitrary")),
        name=name,
    )(a, *([w] * len(col_starts)))


def _epi_id(accs, j):
    return accs[0]


def _epi_glu(accs, j):
    return accs[0] * _sigmoid(accs[1])


def _epi_z_gates(accs, j, *, n_silu_tiles):
    s = _sigmoid(accs[0])
    return jnp.where(j < n_silu_tiles, accs[0] * s, s)


def _qkv_kernel(a_ref, w_ref, gain_ref, o_ref, *scratch, dilation):
    tm = a_ref.shape[0]
    per = tm // dilation
    for c in range(o_ref.shape[3] // CHUNK):
        cs = slice(c * CHUNK, (c + 1) * CHUNK)
        val = jnp.dot(a_ref[...], w_ref[:, cs], preferred_element_type=F32)
        if c * CHUNK < 2 * ATTN_OUT:
            heads = []
            for hh in range(CHUNK // ATTN_HEAD_DIM):
                lo = hh * ATTN_HEAD_DIM
                heads.append(_rms(val[:, lo:lo + ATTN_HEAD_DIM],
                                  gain_ref[:, c * CHUNK + lo:c * CHUNK + lo + ATTN_HEAD_DIM]))
            val = jnp.concatenate(heads, axis=1)
        if dilation == 1:
            o_ref[0, 0, :, cs] = val.astype(o_ref.dtype)
        else:
            scr = scratch[c % len(scratch)]
            for hh in range(CHUNK // LANES):
                scr[hh] = val[:, hh * LANES:(hh + 1) * LANES]
            for r in range(dilation):
                for hh in range(CHUNK // LANES):
                    lo = c * CHUNK + hh * LANES
                    o_ref[0, r, :, lo:lo + LANES] = scr[hh, pl.ds(r, per, stride=dilation), :].astype(o_ref.dtype)


def _qkv_proj(a, w, col_start, gains, bsz, t, dilation, tm=1024):
    m, k = a.shape
    n = 3 * ATTN_OUT
    nt = t // tm
    off = col_start // n
    scratch = [] if dilation == 1 else [pltpu.VMEM((CHUNK // LANES, tm, LANES), F32) for _ in range(2)]
    return pl.pallas_call(
        functools.partial(_qkv_kernel, dilation=dilation),
        grid=(m // tm,),
        in_specs=[pl.BlockSpec((tm, k), lambda i: (i, 0)),
                  pl.BlockSpec((k, n), lambda i: (0, off)),
                  pl.BlockSpec((1, n), lambda i: (0, 0))],
        out_specs=pl.BlockSpec((1, dilation, tm // dilation, n), lambda i: (i // nt, 0, i % nt, 0)),
        out_shape=jax.ShapeDtypeStruct((bsz, dilation, t // dilation, n), BF16),
        scratch_shapes=scratch,
        compiler_params=_params(("parallel",)),
        name=f"proj_qkv_d{dilation}",
    )(a, w, gains)


def _softplus(x):
    return jnp.maximum(x, 0.0) + jnp.log1p(jnp.exp(-jnp.abs(x)))


def _dt_kernel(h_ref, w_ref, wt_ref, b_ref, bt_ref, dt_ref, dtt_ref):
    h = h_ref[...]
    sp = _softplus(jnp.dot(h, w_ref[...], preferred_element_type=F32) + b_ref[...])
    dt_ref[0] = sp[:, :SSD_HEADS]
    dt_ref[1] = sp[:, SSD_HEADS:]
    sp_t = _softplus(lax.dot_general(wt_ref[...], h, (((1,), (1,)), ((), ())),
                                     preferred_element_type=F32) + bt_ref[...])
    for c in range(sp_t.shape[1] // SSD_CHUNK):
        blk = sp_t[:, c * SSD_CHUNK:(c + 1) * SSD_CHUNK]
        dtt_ref[c, 0] = blk[:SSD_HEADS]
        dtt_ref[c, 1] = blk[SSD_HEADS:]


def _dt_proj(h, w_dt, bias, tm=512):
    m, k = h.shape
    nd = 2 * SSD_HEADS
    cpb = tm // SSD_CHUNK
    return pl.pallas_call(
        _dt_kernel,
        grid=(m // tm,),
        in_specs=[pl.BlockSpec((tm, k), lambda i: (i, 0)),
                  pl.BlockSpec((k, nd), lambda i: (0, 0)),
                  pl.BlockSpec((nd, k), lambda i: (0, 0)),
                  pl.BlockSpec((1, nd), lambda i: (0, 0)),
                  pl.BlockSpec((nd, 1), lambda i: (0, 0))],
        out_specs=[pl.BlockSpec((2, tm, SSD_HEADS), lambda i: (0, i, 0)),
                   pl.BlockSpec((cpb, 2, SSD_HEADS, SSD_CHUNK), lambda i: (i, 0, 0, 0))],
        out_shape=[jax.ShapeDtypeStruct((2, m, SSD_HEADS), F32),
                   jax.ShapeDtypeStruct((m // SSD_CHUNK, 2, SSD_HEADS, SSD_CHUNK), F32)],
        compiler_params=_params(("parallel",)),
        name="dt_proj",
    )(h, w_dt, w_dt.T, bias.reshape(1, nd), bias.reshape(nd, 1))


def _dwconv_kernel(cur_ref, left_ref, right_ref, w_ref, b_ref, g_ref, beta_ref, o_ref, pad_ref, y_ref,
                   *, taps, tt, layer_norm):
    i = pl.program_id(1)
    n_i = pl.num_programs(1)
    nct = pad_ref.shape[0]
    lmask = (i > 0).astype(F32)
    rmask = (i < n_i - 1).astype(F32)
    for c in range(nct):
        sl = slice(c * LANES, (c + 1) * LANES)
        pad_ref[c, 0:HALO, :] = left_ref[0, :, sl].astype(F32) * lmask
        pad_ref[c, HALO:HALO + tt, :] = cur_ref[0, :, sl].astype(F32)
        pad_ref[c, HALO + tt:, :] = right_ref[0, :, sl].astype(F32) * rmask

    rows = 64
    first = HALO - taps // 2

    def tile_body(c, carry):
        accs = [jnp.broadcast_to(b_ref[c], (rows, LANES)) for _ in range(tt // rows)]
        for k in range(taps):
            wk = w_ref[c, k:k + 1, :]
            for rc in range(tt // rows):
                accs[rc] = accs[rc] + wk * pad_ref[c, pl.ds(first + k + rc * rows, rows), :]
        for rc in range(tt // rows):
            y_ref[c, rc * rows:(rc + 1) * rows, :] = accs[rc]
        return carry

    lax.fori_loop(0, nct, tile_body, 0)

    if layer_norm:
        tot = y_ref[0]
        for c in range(1, nct):
            tot = tot + y_ref[c]
        mu = jnp.sum(tot, axis=-1, keepdims=True) * (1.0 / (nct * LANES))
        sq = jnp.square(y_ref[0] - mu)
        for c in range(1, nct):
            sq = sq + jnp.square(y_ref[c] - mu)
        rstd = lax.rsqrt(jnp.sum(sq, axis=-1, keepdims=True) * (1.0 / (nct * LANES)) + NORM_EPS)
    for c in range(nct):
        sl = slice(c * LANES, (c + 1) * LANES)
        y = y_ref[c]
        if layer_norm:
            y = (y - mu) * rstd * g_ref[c] + beta_ref[c]
        o_ref[0, :, sl] = (y * _sigmoid(y)).astype(o_ref.dtype)


def _dwconv(x, w, b, g, beta, *, layer_norm, tt=256):
    bsz, t, c = x.shape
    taps = w.shape[0]
    nct = c // LANES
    w3 = w.reshape(taps, nct, LANES).transpose(1, 0, 2)
    tiles = lambda v: v.reshape(nct, 1, LANES)
    hb = tt // HALO
    n_halo = t // HALO
    small = lambda shape: pl.BlockSpec(shape, lambda bb, i: (0,) * len(shape))
    return pl.pallas_call(
        functools.partial(_dwconv_kernel, taps=taps, tt=tt, layer_norm=layer_norm),
        grid=(bsz, t // tt),
        in_specs=[pl.BlockSpec((1, tt, c), lambda bb, i: (bb, i, 0)),
                  pl.BlockSpec((1, HALO, c), lambda bb, i: (bb, jnp.maximum(i * hb - 1, 0), 0)),
                  pl.BlockSpec((1, HALO, c), lambda bb, i: (bb, jnp.minimum((i + 1) * hb, n_halo - 1), 0)),
                  small((nct, taps, LANES)), small((nct, 1, LANES)),
                  small((nct, 1, LANES)), small((nct, 1, LANES))],
        out_specs=pl.BlockSpec((1, tt, c), lambda bb, i: (bb, i, 0)),
        out_shape=jax.ShapeDtypeStruct((bsz, t, c), BF16),
        scratch_shapes=[pltpu.VMEM((nct, tt + 2 * HALO, LANES), F32), pltpu.VMEM((nct, tt, LANES), F32)],
        compiler_params=_params(("parallel", "parallel")),
        name="dwconv_ln" if layer_norm else "dwconv",
    )(x, x, x, w3, tiles(b), tiles(g), tiles(beta))


def _head_block_diag(v, lane):
    zero = jnp.zeros_like(v)
    return jnp.concatenate([jnp.where(lane < SSD_HEAD_DIM, v, zero),
                            jnp.where(lane >= SSD_HEAD_DIM, v, zero)], axis=0)


def _split3(v):
    hi = v.astype(BF16)
    r1 = v - hi.astype(F32)
    mid = r1.astype(BF16)
    lo = (r1 - mid.astype(F32)).astype(BF16)
    return jnp.concatenate([hi, mid, lo], axis=1)


def _ssd_kernel(xs_ref, b_ref, c_ref, dt_ref, dtt_ref, alog_ref, alogc_ref, y_ref, state_ref, neg_ref, *, cpb):
    direction = pl.program_id(1)
    fwd = direction == 0
    q = SSD_CHUNK

    @pl.when(pl.program_id(2) == 0)
    def _():
        state_ref[...] = jnp.zeros_like(state_ref)

    row = lax.broadcasted_iota(jnp.int32, (q, q), 0)
    col = lax.broadcasted_iota(jnp.int32, (q, q), 1)
    sign = jnp.where(fwd, 1, -1)
    mask = (row - col) * sign >= 0
    neg_ref[...] = jnp.where(mask, 0.0, -jnp.inf)
    tri = mask.astype(F32)
    tri_t = ((col - row) * sign >= 0).astype(F32)
    head_of_col = lax.broadcasted_iota(jnp.int32, (3 * SSD_HEADS, D_MODEL), 1) // SSD_HEAD_DIM
    head_of_row = lax.broadcasted_iota(jnp.int32, (3 * SSD_HEADS, D_MODEL), 0) % SSD_HEADS
    expand3 = jnp.where(head_of_col == head_of_row, 1.0, 0.0).astype(BF16)
    lane = lax.broadcasted_iota(jnp.int32, (q, LANES), 1)
    a_row = -jnp.exp(alog_ref[0])
    a_col = -jnp.exp(alogc_ref[0])
    hi = lax.Precision.HIGHEST

    for step in range(cpb):
        ci = jnp.where(fwd, step, cpb - 1 - step)
        r0 = pl.multiple_of(ci * q, q)
        x = xs_ref[0, pl.ds(r0, q), :]
        bm = b_ref[0, pl.ds(r0, q), :]
        cm = c_ref[0, pl.ds(r0, q), :]
        dt = dt_ref[0, 0, pl.ds(r0, q), :]
        dtt = dtt_ref[ci, 0]
        cs = jnp.dot(tri, dt * a_row, precision=hi, preferred_element_type=F32)
        cs_t = jnp.dot(dtt * a_col, tri_t, precision=hi, preferred_element_type=F32)
        tot = jnp.where(fwd, cs[q - 1:q, :], cs[0:1, :])
        scale3 = _split3(jnp.concatenate([dt * jnp.exp(tot - cs), jnp.exp(cs)], axis=0))
        cs2 = cs * LOG2E
        cs2_t = cs_t * LOG2E - jnp.log(dtt) * LOG2E
        for g in range(SSD_GROUPS):
            gs = slice(g * SSD_STATE, (g + 1) * SSD_STATE)
            xg = slice(g * GROUP_W, (g + 1) * GROUP_W)
            bg, cg = bm[:, gs], cm[:, gs]
            scale_x = jnp.dot(scale3, expand3[:, xg], preferred_element_type=F32)
            w_state = scale_x[:q]
            e_in = scale_x[q:]
            decay_row = jnp.where(fwd, e_in[q - 1:q, :], e_in[0:1, :])
            xw = (x[:, xg].astype(F32) * w_state).astype(BF16)
            state = state_ref[:, xg]
            cb = lax.dot_general(cg, bg, (((1,), (1,)), ((), ())), preferred_element_type=F32)
            y_off = jnp.dot(cg, state.astype(BF16), preferred_element_type=F32) * e_in
            bg_t = bg.astype(F32).T.astype(BF16)
            state_ref[:, xg] = state * decay_row + jnp.dot(bg_t, xw, preferred_element_type=F32)
            for pr in range(GROUP_W // LANES):
                t = g * (GROUP_W // LANES) + pr
                ts = slice(t * LANES, (t + 1) * LANES)
                ms = []
                for h in (2 * t, 2 * t + 1):
                    seg = (cs2[:, h:h + 1] - cs2_t[h:h + 1, :]) + neg_ref[...]
                    ms.append((cb * jnp.exp2(seg)).astype(BF16))
                y_diag = jnp.dot(jnp.concatenate(ms, axis=1), _head_block_diag(x[:, ts], lane),
                                 preferred_element_type=F32)
                y_ref[0, 0, pl.ds(r0, q), ts] = (y_diag + y_off[:, pr * LANES:(pr + 1) * LANES]).astype(y_ref.dtype)


def _ssd_scan(xbc_act, dt, dtt, a_log, rows=512):
    bsz, t, _ = xbc_act.shape
    nblk = t // rows
    cpb = rows // SSD_CHUNK
    blk = lambda d, i: jnp.where(d == 0, i, nblk - 1 - i)
    bc0 = D_MODEL // SSD_BC
    return pl.pallas_call(
        functools.partial(_ssd_kernel, cpb=cpb),
        grid=(bsz, 2, nblk),
        in_specs=[pl.BlockSpec((1, rows, D_MODEL), lambda b, d, i: (b, blk(d, i), 0)),
                  pl.BlockSpec((1, rows, SSD_BC), lambda b, d, i: (b, blk(d, i), bc0)),
                  pl.BlockSpec((1, rows, SSD_BC), lambda b, d, i: (b, blk(d, i), bc0 + 1)),
                  pl.BlockSpec((1, 1, rows, SSD_HEADS), lambda b, d, i: (d, b, blk(d, i), 0)),
                  pl.BlockSpec((cpb, 1, SSD_HEADS, SSD_CHUNK),
                               lambda b, d, i: (b * nblk + blk(d, i), d, 0, 0)),
                  pl.BlockSpec((1, 1, SSD_HEADS), lambda b, d, i: (d, 0, 0)),
                  pl.BlockSpec((1, SSD_HEADS, 1), lambda b, d, i: (d, 0, 0))],
        out_specs=pl.BlockSpec((1, 1, rows, D_MODEL), lambda b, d, i: (d, b, blk(d, i), 0)),
        out_shape=jax.ShapeDtypeStruct((2, bsz, t, D_MODEL), BF16),
        scratch_shapes=[pltpu.VMEM((SSD_STATE, D_MODEL), F32), pltpu.VMEM((SSD_CHUNK, SSD_CHUNK), F32)],
        compiler_params=_params(("parallel", "parallel", "arbitrary")),
        name="ssd_scan",
    )(xbc_act, xbc_act, xbc_act, dt.reshape(2, bsz, t, SSD_HEADS), dtt,
      a_log.reshape(2, 1, SSD_HEADS), a_log.reshape(2, SSD_HEADS, 1))


def _attn_kernel(q_ref, k_ref, v_ref, slope_ref, o_ref, lse_ref, *, sub, dilation, radius, qb, kw):
    def body(qi, carry):
        q0 = pl.multiple_of(qi * qb, qb)
        k0 = pl.multiple_of(jnp.clip(q0 - radius, 0, sub - kw), radius)
        qpos = q0 + lax.broadcasted_iota(jnp.int32, (qb, kw), 0)
        kpos = k0 + lax.broadcasted_iota(jnp.int32, (qb, kw), 1)
        dist = jnp.abs(kpos - qpos)
        valid = dist <= radius
        neg_dist = dist.astype(F32) * (-float(dilation))
        for h in range(ATTN_HPG):
            hs = slice(h * ATTN_HEAD_DIM, (h + 1) * ATTN_HEAD_DIM)
            q = q_ref[0, 0, pl.ds(q0, qb), hs]
            k = k_ref[0, 0, pl.ds(k0, kw), hs]
            v = v_ref[0, 0, pl.ds(k0, kw), hs]
            s = lax.dot_general(q, k, (((1,), (1,)), ((), ())), preferred_element_type=F32)
            s = jnp.where(valid, s + slope_ref[0, h:h + 1, :1] * neg_dist, NEG_INF)
            m = jnp.max(s, axis=-1, keepdims=True)
            p = jnp.exp(s - m)
            l = jnp.sum(p, axis=-1, keepdims=True)
            o = jnp.dot(p.astype(BF16), v, preferred_element_type=F32)
            o_ref[0, 0, pl.ds(q0, qb), hs] = (o / l).astype(o_ref.dtype)
            lse_ref[0, 0, pl.ds(q0, qb), hs] = jnp.broadcast_to(m + jnp.log(l), (qb, LANES))
        return carry

    lax.fori_loop(0, sub // qb, body, 0)


def _attention_group(qkv, slopes, group, window, dilation):
    bsz, _, sub, _ = qkv.shape
    radius = window // (2 * dilation)
    qb = min(2 * radius, sub)
    kw = min(qb + 2 * radius, sub)
    spec = lambda part: pl.BlockSpec((1, 1, sub, ATTN_OUT), lambda b, r: (b, r, 0, part))
    out_spec = pl.BlockSpec((1, 1, sub, ATTN_OUT), lambda b, r: (b, r, 0, 0))
    return pl.pallas_call(
        functools.partial(_attn_kernel, sub=sub, dilation=dilation, radius=radius, qb=qb, kw=kw),
        grid=(bsz, dilation),
        in_specs=[spec(0), spec(1), spec(2),
                  pl.BlockSpec((1, ATTN_HPG, LANES), lambda b, r: (group, 0, 0))],
        out_specs=[out_spec, out_spec],
        out_shape=[jax.ShapeDtypeStruct((bsz, dilation, sub, ATTN_OUT), BF16),
                   jax.ShapeDtypeStruct((bsz, dilation, sub, ATTN_OUT), F32)],
        compiler_params=_params(("parallel", "parallel")),
        name=f"attn_d{dilation}",
    )(qkv, qkv, qkv, slopes)


def _branch_kernel(yf_ref, yb_ref, xs_ref, sz_ref, d_ref, gs_ref,
                   o1_ref, o2_ref, o3_ref, l1_ref, l2_ref, l3_ref, yc_ref,
                   g1_ref, g2_ref, g3_ref, ws_ref, wa_ref, wc_ref, out_ref,
                   ys_scr, ya_scr, o2_scr, o3_scr, l2_scr, l3_scr):
    @pl.when(pl.program_id(1) == 0)
    def _():
        y = yf_ref[...].astype(F32) + yb_ref[...].astype(F32) + d_ref[...] * xs_ref[...].astype(F32)
        y = y * sz_ref[...].astype(F32)
        ys_scr[...] = _rms(y, gs_ref[...]).astype(BF16)
        for src_o, src_l, dst_o, dst_l in ((o2_ref, l2_ref, o2_scr, l2_scr), (o3_ref, l3_ref, o3_scr, l3_scr)):
            dil, per = src_o.shape[1], src_o.shape[2]
            for r in range(dil):
                for hh in range(ATTN_HPG):
                    hs = slice(hh * LANES, (hh + 1) * LANES)
                    dst_o[hh, pl.ds(r, per, stride=dil), :] = src_o[0, r, :, hs].astype(F32)
                    dst_l[hh, pl.ds(r, per, stride=dil), :] = src_l[0, r, :, hs]
        for hh in range(ATTN_HPG):
            hs = slice(hh * LANES, (hh + 1) * LANES)
            l1, l2, l3 = l1_ref[0, 0, :, hs], l2_scr[hh], l3_scr[hh]
            mx = jnp.maximum(jnp.maximum(l1, l2), l3)
            w1, w2, w3 = jnp.exp(l1 - mx), jnp.exp(l2 - mx), jnp.exp(l3 - mx)
            num = w1 * o1_ref[0, 0, :, hs].astype(F32) + w2 * o2_scr[hh] + w3 * o3_scr[hh]
            ya_scr[:, hs] = (num / (w1 + w2 + w3)).astype(BF16)

    for c in range(out_ref.shape[1] // CHUNK):
        cs = slice(c * CHUNK, (c + 1) * CHUNK)
        acc = g1_ref[:, cs].astype(F32) * jnp.dot(ys_scr[...], ws_ref[:, cs], preferred_element_type=F32)
        acc += g2_ref[:, cs].astype(F32) * jnp.dot(ya_scr[...], wa_ref[:, cs], preferred_element_type=F32)
        acc += g3_ref[:, cs].astype(F32) * jnp.dot(yc_ref[...], wc_ref[:, cs], preferred_element_type=F32)
        out_ref[:, cs] = acc.astype(out_ref.dtype)


def _branch_merge(y_dirs, xbc_act, zg, d_exp, g_ssd, att, yc, w_s, w_a, w_c, t, tm=512, tn=512):
    m = zg.shape[0]
    d = D_MODEL
    nj = d // tn
    nt = t // tm
    row = lambda w: pl.BlockSpec((tm, w), lambda i, j: (i, 0))
    vec = pl.BlockSpec((1, d), lambda i, j: (0, 0))
    gate = lambda g: pl.BlockSpec((tm, tn), lambda i, j: (i, (P_GATE + g * d) // tn + j))
    wspec = lambda k: pl.BlockSpec((k, tn), lambda i, j: (0, j))
    att_spec = lambda dil: pl.BlockSpec((1, dil, tm // dil, ATTN_OUT), lambda i, j: (i // nt, 0, i % nt, 0))
    (o1, l1), (o2, l2), (o3, l3) = att
    dils = [dil for _, dil in ATTN_PATTERNS]
    return pl.pallas_call(
        _branch_kernel,
        grid=(m // tm, nj),
        in_specs=[row(d), row(d), row(d), row(d), vec, vec,
                  att_spec(dils[0]), att_spec(dils[1]), att_spec(dils[2]),
                  att_spec(dils[0]), att_spec(dils[1]), att_spec(dils[2]),
                  row(d), gate(0), gate(1), gate(2), wspec(d), wspec(ATTN_OUT), wspec(d)],
        out_specs=pl.BlockSpec((tm, tn), lambda i, j: (i, j)),
        out_shape=jax.ShapeDtypeStruct((m, d), BF16),
        scratch_shapes=[pltpu.VMEM((tm, d), BF16), pltpu.VMEM((tm, ATTN_OUT), BF16)]
        + [pltpu.VMEM((ATTN_HPG, tm, LANES), F32) for _ in range(4)],
        compiler_params=_params(("parallel", "arbitrary")),
        name="branch_merge",
    )(y_dirs[0], y_dirs[1], xbc_act, zg, d_exp, g_ssd, o1, o2, o3, l1, l2, l3, yc,
      zg, zg, zg, w_s, w_a, w_c)


def _out_kernel(m_ref, w_ref, x_ref, g_ref, xo_ref, h_ref):
    for c in range(xo_ref.shape[1] // CHUNK):
        cs = slice(c * CHUNK, (c + 1) * CHUNK)
        xo_ref[:, cs] = x_ref[:, cs] + jnp.dot(m_ref[...], w_ref[:, cs], preferred_element_type=F32)
    h_ref[...] = _rms(xo_ref[...], g_ref[...]).astype(h_ref.dtype)


def _out_proj(merged, w_out, x, g2, tm=256):
    m, d = x.shape
    row = pl.BlockSpec((tm, d), lambda i: (i, 0))
    return pl.pallas_call(
        _out_kernel,
        grid=(m // tm,),
        in_specs=[row, pl.BlockSpec((d, d), lambda i: (0, 0)), row, pl.BlockSpec((1, d), lambda i: (0, 0))],
        out_specs=[row, row],
        out_shape=[jax.ShapeDtypeStruct((m, d), F32), jax.ShapeDtypeStruct((m, d), BF16)],
        compiler_params=_params(("parallel",)),
        name="out_proj",
    )(merged, w_out, x, g2.reshape(1, d))


def _mlp_kernel(h_ref, wu_ref, wd_ref, x_ref, g_ref, xo_ref, hn_ref, acc_ref):
    j = pl.program_id(1)

    @pl.when(j == 0)
    def _():
        acc_ref[...] = x_ref[...]

    us = []
    for c in range(wu_ref.shape[1] // CHUNK):
        u = jnp.maximum(jnp.dot(h_ref[...], wu_ref[:, c * CHUNK:(c + 1) * CHUNK], preferred_element_type=F32), 0.0)
        us.append((u * u).astype(BF16))
    u = jnp.concatenate(us, axis=1)
    for c in range(acc_ref.shape[1] // CHUNK):
        cs = slice(c * CHUNK, (c + 1) * CHUNK)
        acc_ref[:, cs] += jnp.dot(u, wd_ref[:, cs], preferred_element_type=F32)

    @pl.when(j == pl.num_programs(1) - 1)
    def _():
        xn = acc_ref[...]
        xo_ref[...] = xn
        hn_ref[...] = _rms(xn, g_ref[...]).astype(hn_ref.dtype)


def _mlp(h2, w_up, w_down, x, g_next, tm=512, th=1024):
    m, d = x.shape
    hid = w_up.shape[1]
    row = pl.BlockSpec((tm, d), lambda i, j: (i, 0))
    return pl.pallas_call(
        _mlp_kernel,
        grid=(m // tm, hid // th),
        in_specs=[row, pl.BlockSpec((d, th), lambda i, j: (0, j)), pl.BlockSpec((th, d), lambda i, j: (j, 0)),
                  row, pl.BlockSpec((1, d), lambda i, j: (0, 0))],
        out_specs=[row, row],
        out_shape=[jax.ShapeDtypeStruct((m, d), F32), jax.ShapeDtypeStruct((m, d), BF16)],
        scratch_shapes=[pltpu.VMEM((tm, d), F32)],
        compiler_params=_params(("parallel", "arbitrary")),
        name="mlp",
    )(h2, w_up, w_down, x, g_next.reshape(1, d))


PACK_W = 512


def _pack_kernel(tbl_ref, a_ref, b_ref, o_ref, *, first_shifted, shift):
    src = tbl_ref[pl.program_id(0)]

    @pl.when(src < first_shifted)
    def _():
        o_ref[...] = a_ref[...].astype(o_ref.dtype)

    @pl.when(src >= first_shifted)
    def _():
        o_ref[:, :PACK_W - shift] = a_ref[:, shift:].astype(o_ref.dtype)
        o_ref[:, PACK_W - shift:] = b_ref[:, :shift].astype(o_ref.dtype)


def _pack_in_proj(w):
    k, n_src = w.shape
    shift = COL_Q - COL_DT
    blk = lambda col, width: [col // PACK_W + b for b in range(width // PACK_W)]
    order = blk(COL_Z, D_MODEL) + blk(COL_GATE - shift, N_BRANCHES * D_MODEL) + blk(COL_GLU - shift, 2 * D_MODEL)
    order += blk(COL_XBC, SSD_XBC)
    for g in range(len(ATTN_PATTERNS)):
        for base in (COL_Q, COL_K, COL_V):
            order += blk(base - shift + g * ATTN_OUT, ATTN_OUT)
    table = jnp.asarray(order, jnp.int32)
    sub = PACK_W // LANES
    packed = pl.pallas_call(
        functools.partial(_pack_kernel, first_shifted=COL_DT // PACK_W, shift=shift),
        grid_spec=pltpu.PrefetchScalarGridSpec(
            num_scalar_prefetch=1,
            grid=(len(order),),
            in_specs=[pl.BlockSpec((k, PACK_W), lambda j, tbl: (0, tbl[j])),
                      pl.BlockSpec((k, LANES), lambda j, tbl: (0, (tbl[j] + 1) * sub))],
            out_specs=pl.BlockSpec((k, PACK_W), lambda j, tbl: (0, j))),
        out_shape=jax.ShapeDtypeStruct((k, len(order) * PACK_W), BF16),
        compiler_params=_params(("arbitrary",)),
        name="pack_w_in",
    )(table, w, w)
    return packed, w[:, COL_DT:COL_Q].astype(BF16)


def kernel(x, norm1_g, w_in, ssd_conv_w, ssd_conv_b, ssd_dt_bias, ssd_a_log, ssd_d, ssd_norm_g, w_ssd_o,
           q_norm_g, k_norm_g, w_attn_o, conv_dw_w, conv_dw_b, conv_ln_g, conv_ln_b, w_conv_o, w_out,
           norm2_g, w_mlp_up, w_mlp_down):
    bsz, t, d = x.shape
    m = bsz * t
    depth = w_in.shape[0]
    xf = x.reshape(m, d)
    slopes = jnp.exp2(-ALIBI_MAX_EXP * jnp.arange(1, ATTN_HEADS + 1, dtype=F32) / ATTN_HEADS)
    slopes = jnp.broadcast_to(slopes.reshape(len(ATTN_PATTERNS), ATTN_HPG, 1),
                              (len(ATTN_PATTERNS), ATTN_HPG, LANES))
    ones_c = jnp.ones((SSD_XBC,), F32)
    zeros_c = jnp.zeros((SSD_XBC,), F32)
    h = _norm(xf, norm1_g[0])
    for i in range(depth):
        w, w_dt = _pack_in_proj(w_in[i])
        zg = _mm(h, w, [P_Z], P_GLU_A, functools.partial(_epi_z_gates, n_silu_tiles=D_MODEL // 1024),
                 1024, 1024, "proj_z_gates")
        glu = _mm(h, w, [P_GLU_A, P_GLU_G], D_MODEL, _epi_glu, 1024, 1024, "proj_glu")
        xbc = _mm(h, w, [P_XBC], SSD_XBC, _epi_id, 1024, 1024, "proj_xbc")
        dt, dtt = _dt_proj(h, w_dt, ssd_dt_bias[i].reshape(-1))
        qk_gain = jnp.concatenate([jnp.tile(q_norm_g[i] * ATTN_HEAD_DIM ** -0.5, ATTN_HPG),
                                   jnp.tile(k_norm_g[i], ATTN_HPG),
                                   jnp.ones((ATTN_OUT,), F32)]).reshape(1, 3 * ATTN_OUT)
        att = []
        for g, (window, dilation) in enumerate(ATTN_PATTERNS):
            qkv = _qkv_proj(h, w, P_QKV + g * 3 * ATTN_OUT, qk_gain, bsz, t, dilation)
            att.append(_attention_group(qkv, slopes, g, window, dilation))

        xbc_act = _dwconv(xbc.reshape(bsz, t, SSD_XBC), ssd_conv_w[i], ssd_conv_b[i], ones_c, zeros_c,
                          layer_norm=False)
        y_dirs = _ssd_scan(xbc_act, dt, dtt, ssd_a_log[i])
        yc = _dwconv(glu.reshape(bsz, t, D_MODEL), conv_dw_w[i], conv_dw_b[i], conv_ln_g[i], conv_ln_b[i],
                     layer_norm=True)

        merged = _branch_merge(
            y_dirs.reshape(2, m, d), xbc_act.reshape(m, SSD_XBC), zg,
            jnp.repeat(ssd_d[i], SSD_HEAD_DIM).reshape(1, d), ssd_norm_g[i].reshape(1, d),
            att, yc.reshape(m, d),
            w_ssd_o[i].astype(BF16), w_attn_o[i].astype(BF16), w_conv_o[i].astype(BF16), t)
        xf, h2 = _out_proj(merged, w_out[i].astype(BF16), xf, norm2_g[i])
        g_next = norm1_g[i + 1] if i + 1 < depth else norm1_g[i]
        xf, h = _mlp(h2, w_mlp_up[i].astype(BF16), w_mlp_down[i].astype(BF16), xf, g_next)
    return xf.reshape(bsz, t, d)
```

```python
import functools
import math

import jax
import jax.numpy as jnp
from jax import lax
from jax.experimental import pallas as pl
from jax.experimental.pallas import tpu as pltpu

F32 = jnp.float32
BF16 = jnp.bfloat16

D_MODEL = 2048
SSD_HEAD_DIM = 64
SSD_HEADS = D_MODEL // SSD_HEAD_DIM
SSD_GROUPS = 4
SSD_STATE = 128
SSD_CHUNK = 128
SSD_BC = SSD_GROUPS * SSD_STATE
SSD_XBC = D_MODEL + 2 * SSD_BC
GROUP_W = D_MODEL // SSD_GROUPS
ATTN_HEAD_DIM = 128
ATTN_PATTERNS = ((128, 1), (512, 4), (2048, 16))
ATTN_HPG = 4
ATTN_HEADS = ATTN_HPG * len(ATTN_PATTERNS)
ATTN_WIDTH = ATTN_HEADS * ATTN_HEAD_DIM
ATTN_OUT = ATTN_HPG * ATTN_HEAD_DIM
ALIBI_MAX_EXP = 8.0
NEG_INF = -1e30
CONV_WIDTH = 31
MLP_HIDDEN = 4 * D_MODEL
NORM_EPS = 1e-6
N_BRANCHES = 3

COL_Z = 0
COL_XBC = COL_Z + D_MODEL
COL_DT = COL_XBC + SSD_XBC
COL_Q = COL_DT + 2 * SSD_HEADS
COL_K = COL_Q + ATTN_WIDTH
COL_V = COL_K + ATTN_WIDTH
COL_GLU = COL_V + ATTN_WIDTH
COL_GATE = COL_GLU + 2 * D_MODEL

P_Z = 0
P_GATE = P_Z + D_MODEL
P_GLU_A = P_GATE + N_BRANCHES * D_MODEL
P_GLU_G = P_GLU_A + D_MODEL
P_XBC = P_GLU_G + D_MODEL
P_QKV = P_XBC + SSD_XBC

LANES = 128
HALO = 16
VMEM_LIMIT = 56 * 1024 * 1024
LOG2E = math.log2(math.e)


def _params(sem):
    return pltpu.CompilerParams(dimension_semantics=sem, vmem_limit_bytes=VMEM_LIMIT)


def _rms(x, g):
    return x * lax.rsqrt(jnp.mean(x * x, axis=-1, keepdims=True) + NORM_EPS) * g


def _sigmoid(x):
    return 0.5 * jnp.tanh(0.5 * x) + 0.5


def _norm_kernel(x_ref, g_ref, o_ref):
    o_ref[...] = _rms(x_ref[...], g_ref[...]).astype(o_ref.dtype)


def _norm(x, g, tm=512):
    m, d = x.shape
    return pl.pallas_call(
        _norm_kernel,
        grid=(m // tm,),
        in_specs=[pl.BlockSpec((tm, d), lambda i: (i, 0)), pl.BlockSpec((1, d), lambda i: (0, 0))],
        out_specs=pl.BlockSpec((tm, d), lambda i: (i, 0)),
        out_shape=jax.ShapeDtypeStruct((m, d), BF16),
        compiler_params=_params(("parallel",)),
        name="rmsnorm",
    )(x, g.reshape(1, d))


CHUNK = 256


def _mm_kernel(*refs, nw, epilogue):
    a_ref = refs[0]
    w_refs = refs[1:1 + nw]
    o_ref = refs[-1]
    j = pl.program_id(1)
    for c in range(o_ref.shape[1] // CHUNK):
        cs = slice(c * CHUNK, (c + 1) * CHUNK)
        accs = [jnp.dot(a_ref[...], w[:, cs], preferred_element_type=F32) for w in w_refs]
        o_ref[:, cs] = epilogue(accs, j).astype(o_ref.dtype)


def _mm(a, w, layer, col_starts, n, epilogue, tm, tn, name):
    m, k = a.shape
    in_specs = [pl.BlockSpec((tm, k), lambda i, j: (i, 0))]
    for s in col_starts:
        in_specs.append(pl.BlockSpec((None, k, tn),
                                     functools.partial(lambda i, j, off: (layer, 0, off + j), off=s // tn)))
    return pl.pallas_call(
        functools.partial(_mm_kernel, nw=len(col_starts), epilogue=epilogue),
        grid=(m // tm, n // tn),
        in_specs=in_specs,
        out_specs=pl.BlockSpec((tm, tn), lambda i, j: (i, j)),
        out_shape=jax.ShapeDtypeStruct((m, n), BF16),
        compiler_params=_params(("parallel", "arbitrary")),
        name=name,
    )(a, *([w] * len(col_starts)))


def _epi_id(accs, j):
    return accs[0]


def _epi_glu(accs, j):
    return accs[0] * _sigmoid(accs[1])


def _epi_z_gates(accs, j, *, n_silu_tiles):
    s = _sigmoid(accs[0])
    return jnp.where(j < n_silu_tiles, accs[0] * s, s)


def _qkv_kernel(a_ref, w_ref, gain_ref, o_ref, *scratch, dilation):
    tm = a_ref.shape[0]
    per = tm // dilation
    for c in range(o_ref.shape[3] // CHUNK):
        cs = slice(c * CHUNK, (c + 1) * CHUNK)
        val = jnp.dot(a_ref[...], w_ref[:, cs], preferred_element_type=F32)
        if c * CHUNK < 2 * ATTN_OUT:
            heads = []
            for hh in range(CHUNK // ATTN_HEAD_DIM):
                lo = hh * ATTN_HEAD_DIM
                heads.append(_rms(val[:, lo:lo + ATTN_HEAD_DIM],
                                  gain_ref[:, c * CHUNK + lo:c * CHUNK + lo + ATTN_HEAD_DIM]))
            val = jnp.concatenate(heads, axis=1)
        if dilation == 1:
            o_ref[0, 0, :, cs] = val.astype(o_ref.dtype)
        else:
            scr = scratch[c % len(scratch)]
            for hh in range(CHUNK // LANES):
                scr[hh] = val[:, hh * LANES:(hh + 1) * LANES]
            for r in range(dilation):
                for hh in range(CHUNK // LANES):
                    lo = c * CHUNK + hh * LANES
                    o_ref[0, r, :, lo:lo + LANES] = scr[hh, pl.ds(r, per, stride=dilation), :].astype(o_ref.dtype)


def _qkv_proj(a, w, layer, col_start, gains, bsz, t, dilation, tm=1024):
    m, k = a.shape
    n = 3 * ATTN_OUT
    nt = t // tm
    off = col_start // n
    scratch = [] if dilation == 1 else [pltpu.VMEM((CHUNK // LANES, tm, LANES), F32) for _ in range(2)]
    return pl.pallas_call(
        functools.partial(_qkv_kernel, dilation=dilation),
        grid=(m // tm,),
        in_specs=[pl.BlockSpec((tm, k), lambda i: (i, 0)),
                  pl.BlockSpec((None, k, n), lambda i: (layer, 0, off)),
                  pl.BlockSpec((1, n), lambda i: (0, 0))],
        out_specs=pl.BlockSpec((1, dilation, tm // dilation, n), lambda i: (i // nt, 0, i % nt, 0)),
        out_shape=jax.ShapeDtypeStruct((bsz, dilation, t // dilation, n), BF16),
        scratch_shapes=scratch,
        compiler_params=_params(("parallel",)),
        name=f"proj_qkv_d{dilation}",
    )(a, w, gains)


def _softplus(x):
    return jnp.maximum(x, 0.0) + jnp.log1p(jnp.exp(-jnp.abs(x)))


def _dt_kernel(h_ref, w_ref, wt_ref, b_ref, bt_ref, dt_ref, dtt_ref):
    h = h_ref[...]
    sp = _softplus(jnp.dot(h, w_ref[...], preferred_element_type=F32) + b_ref[...])
    dt_ref[0] = sp[:, :SSD_HEADS]
    dt_ref[1] = sp[:, SSD_HEADS:]
    sp_t = _softplus(lax.dot_general(wt_ref[...], h, (((1,), (1,)), ((), ())),
                                     preferred_element_type=F32) + bt_ref[...])
    for c in range(sp_t.shape[1] // SSD_CHUNK):
        blk = sp_t[:, c * SSD_CHUNK:(c + 1) * SSD_CHUNK]
        dtt_ref[c, 0] = blk[:SSD_HEADS]
        dtt_ref[c, 1] = blk[SSD_HEADS:]


def _dt_proj(h, w_dt, bias, tm=512):
    m, k = h.shape
    nd = 2 * SSD_HEADS
    cpb = tm // SSD_CHUNK
    return pl.pallas_call(
        _dt_kernel,
        grid=(m // tm,),
        in_specs=[pl.BlockSpec((tm, k), lambda i: (i, 0)),
                  pl.BlockSpec((k, nd), lambda i: (0, 0)),
                  pl.BlockSpec((nd, k), lambda i: (0, 0)),
                  pl.BlockSpec((1, nd), lambda i: (0, 0)),
                  pl.BlockSpec((nd, 1), lambda i: (0, 0))],
        out_specs=[pl.BlockSpec((2, tm, SSD_HEADS), lambda i: (0, i, 0)),
                   pl.BlockSpec((cpb, 2, SSD_HEADS, SSD_CHUNK), lambda i: (i, 0, 0, 0))],
        out_shape=[jax.ShapeDtypeStruct((2, m, SSD_HEADS), F32),
                   jax.ShapeDtypeStruct((m // SSD_CHUNK, 2, SSD_HEADS, SSD_CHUNK), F32)],
        compiler_params=_params(("parallel",)),
        name="dt_proj",
    )(h, w_dt, w_dt.T, bias.reshape(1, nd), bias.reshape(nd, 1))


def _dwconv_kernel(cur_ref, left_ref, right_ref, w_ref, b_ref, g_ref, beta_ref, o_ref, pad_ref, y_ref,
                   *, taps, tt, layer_norm):
    i = pl.program_id(1)
    n_i = pl.num_programs(1)
    nct = pad_ref.shape[0]
    lmask = (i > 0).astype(F32)
    rmask = (i < n_i - 1).astype(F32)
    for c in range(nct):
        sl = slice(c * LANES, (c + 1) * LANES)
        pad_ref[c, 0:HALO, :] = left_ref[0, :, sl].astype(F32) * lmask
        pad_ref[c, HALO:HALO + tt, :] = cur_ref[0, :, sl].astype(F32)
        pad_ref[c, HALO + tt:, :] = right_ref[0, :, sl].astype(F32) * rmask

    rows = 64
    first = HALO - taps // 2

    def tile_body(c, carry):
        accs = [jnp.broadcast_to(b_ref[c], (rows, LANES)) for _ in range(tt // rows)]
        for k in range(taps):
            wk = w_ref[c, k:k + 1, :]
            for rc in range(tt // rows):
                accs[rc] = accs[rc] + wk * pad_ref[c, pl.ds(first + k + rc * rows, rows), :]
        for rc in range(tt // rows):
            y_ref[c, rc * rows:(rc + 1) * rows, :] = accs[rc]
        return carry

    lax.fori_loop(0, nct, tile_body, 0)

    if layer_norm:
        tot = y_ref[0]
        for c in range(1, nct):
            tot = tot + y_ref[c]
        mu = jnp.sum(tot, axis=-1, keepdims=True) * (1.0 / (nct * LANES))
        sq = jnp.square(y_ref[0] - mu)
        for c in range(1, nct):
            sq = sq + jnp.square(y_ref[c] - mu)
        rstd = lax.rsqrt(jnp.sum(sq, axis=-1, keepdims=True) * (1.0 / (nct * LANES)) + NORM_EPS)
    for c in range(nct):
        sl = slice(c * LANES, (c + 1) * LANES)
        y = y_ref[c]
        if layer_norm:
            y = (y - mu) * rstd * g_ref[c] + beta_ref[c]
        o_ref[0, :, sl] = (y * _sigmoid(y)).astype(o_ref.dtype)


def _dwconv(x, w, b, g, beta, *, layer_norm, tt=256):
    bsz, t, c = x.shape
    taps = w.shape[0]
    nct = c // LANES
    w3 = w.reshape(taps, nct, LANES).transpose(1, 0, 2)
    tiles = lambda v: v.reshape(nct, 1, LANES)
    hb = tt // HALO
    n_halo = t // HALO
    small = lambda shape: pl.BlockSpec(shape, lambda bb, i: (0,) * len(shape))
    return pl.pallas_call(
        functools.partial(_dwconv_kernel, taps=taps, tt=tt, layer_norm=layer_norm),
        grid=(bsz, t // tt),
        in_specs=[pl.BlockSpec((1, tt, c), lambda bb, i: (bb, i, 0)),
                  pl.BlockSpec((1, HALO, c), lambda bb, i: (bb, jnp.maximum(i * hb - 1, 0), 0)),
                  pl.BlockSpec((1, HALO, c), lambda bb, i: (bb, jnp.minimum((i + 1) * hb, n_halo - 1), 0)),
                  small((nct, taps, LANES)), small((nct, 1, LANES)),
                  small((nct, 1, LANES)), small((nct, 1, LANES))],
        out_specs=pl.BlockSpec((1, tt, c), lambda bb, i: (bb, i, 0)),
        out_shape=jax.ShapeDtypeStruct((bsz, t, c), BF16),
        scratch_shapes=[pltpu.VMEM((nct, tt + 2 * HALO, LANES), F32), pltpu.VMEM((nct, tt, LANES), F32)],
        compiler_params=_params(("parallel", "parallel")),
        name="dwconv_ln" if layer_norm else "dwconv",
    )(x, x, x, w3, tiles(b), tiles(g), tiles(beta))


def _head_block_diag(v, lane):
    zero = jnp.zeros_like(v)
    return jnp.concatenate([jnp.where(lane < SSD_HEAD_DIM, v, zero),
                            jnp.where(lane >= SSD_HEAD_DIM, v, zero)], axis=0)


def _split3(v):
    hi = v.astype(BF16)
    r1 = v - hi.astype(F32)
    mid = r1.astype(BF16)
    lo = (r1 - mid.astype(F32)).astype(BF16)
    return jnp.concatenate([hi, mid, lo], axis=1)


def _ssd_kernel(xs_ref, b_ref, c_ref, dt_ref, dtt_ref, alog_ref, alogc_ref, y_ref, state_ref, neg_ref, *, cpb):
    direction = pl.program_id(1)
    fwd = direction == 0
    q = SSD_CHUNK

    @pl.when(pl.program_id(2) == 0)
    def _():
        state_ref[...] = jnp.zeros_like(state_ref)

    row = lax.broadcasted_iota(jnp.int32, (q, q), 0)
    col = lax.broadcasted_iota(jnp.int32, (q, q), 1)
    sign = jnp.where(fwd, 1, -1)
    mask = (row - col) * sign >= 0
    neg_ref[...] = jnp.where(mask, 0.0, -jnp.inf)
    tri = mask.astype(F32)
    tri_t = ((col - row) * sign >= 0).astype(F32)
    head_of_col = lax.broadcasted_iota(jnp.int32, (3 * SSD_HEADS, D_MODEL), 1) // SSD_HEAD_DIM
    head_of_row = lax.broadcasted_iota(jnp.int32, (3 * SSD_HEADS, D_MODEL), 0) % SSD_HEADS
    expand3 = jnp.where(head_of_col == head_of_row, 1.0, 0.0).astype(BF16)
    lane = lax.broadcasted_iota(jnp.int32, (q, LANES), 1)
    a_row = -jnp.exp(alog_ref[0])
    a_col = -jnp.exp(alogc_ref[0])
    hi = lax.Precision.HIGHEST

    for step in range(cpb):
        ci = jnp.where(fwd, step, cpb - 1 - step)
        r0 = pl.multiple_of(ci * q, q)
        x = xs_ref[0, pl.ds(r0, q), :]
        bm = b_ref[0, pl.ds(r0, q), :]
        cm = c_ref[0, pl.ds(r0, q), :]
        dt = dt_ref[0, 0, pl.ds(r0, q), :]
        dtt = dtt_ref[ci, 0]
        cs = jnp.dot(tri, dt * a_row, precision=hi, preferred_element_type=F32)
        cs_t = jnp.dot(dtt * a_col, tri_t, precision=hi, preferred_element_type=F32)
        tot = jnp.where(fwd, cs[q - 1:q, :], cs[0:1, :])
        scale3 = _split3(jnp.concatenate([dt * jnp.exp(tot - cs), jnp.exp(cs)], axis=0))
        cs2 = cs * LOG2E
        cs2_t = cs_t * LOG2E - jnp.log(dtt) * LOG2E
        for g in range(SSD_GROUPS):
            gs = slice(g * SSD_STATE, (g + 1) * SSD_STATE)
            xg = slice(g * GROUP_W, (g + 1) * GROUP_W)
            bg, cg = bm[:, gs], cm[:, gs]
            scale_x = jnp.dot(scale3, expand3[:, xg], preferred_element_type=F32)
            w_state = scale_x[:q]
            e_in = scale_x[q:]
            decay_row = jnp.where(fwd, e_in[q - 1:q, :], e_in[0:1, :])
            xw = (x[:, xg].astype(F32) * w_state).astype(BF16)
            state = state_ref[:, xg]
            cb = lax.dot_general(cg, bg, (((1,), (1,)), ((), ())), preferred_element_type=F32)
            y_off = jnp.dot(cg, state.astype(BF16), preferred_element_type=F32) * e_in
            bg_t = bg.astype(F32).T.astype(BF16)
            state_ref[:, xg] = state * decay_row + jnp.dot(bg_t, xw, preferred_element_type=F32)
            for pr in range(GROUP_W // LANES):
                t = g * (GROUP_W // LANES) + pr
                ts = slice(t * LANES, (t + 1) * LANES)
                ms = []
                for h in (2 * t, 2 * t + 1):
                    seg = (cs2[:, h:h + 1] - cs2_t[h:h + 1, :]) + neg_ref[...]
                    ms.append((cb * jnp.exp2(seg)).astype(BF16))
                y_diag = jnp.dot(jnp.concatenate(ms, axis=1), _head_block_diag(x[:, ts], lane),
                                 preferred_element_type=F32)
                y_ref[0, 0, pl.ds(r0, q), ts] = (y_diag + y_off[:, pr * LANES:(pr + 1) * LANES]).astype(y_ref.dtype)


def _ssd_scan(xbc_act, dt, dtt, a_log, rows=512):
    bsz, t, _ = xbc_act.shape
    nblk = t // rows
    cpb = rows // SSD_CHUNK
    blk = lambda d, i: jnp.where(d == 0, i, nblk - 1 - i)
    bc0 = D_MODEL // SSD_BC
    return pl.pallas_call(
        functools.partial(_ssd_kernel, cpb=cpb),
        grid=(bsz, 2, nblk),
        in_specs=[pl.BlockSpec((1, rows, D_MODEL), lambda b, d, i: (b, blk(d, i), 0)),
                  pl.BlockSpec((1, rows, SSD_BC), lambda b, d, i: (b, blk(d, i), bc0)),
                  pl.BlockSpec((1, rows, SSD_BC), lambda b, d, i: (b, blk(d, i), bc0 + 1)),
                  pl.BlockSpec((1, 1, rows, SSD_HEADS), lambda b, d, i: (d, b, blk(d, i), 0)),
                  pl.BlockSpec((cpb, 1, SSD_HEADS, SSD_CHUNK),
                               lambda b, d, i: (b * nblk + blk(d, i), d, 0, 0)),
                  pl.BlockSpec((1, 1, SSD_HEADS), lambda b, d, i: (d, 0, 0)),
                  pl.BlockSpec((1, SSD_HEADS, 1), lambda b, d, i: (d, 0, 0))],
        out_specs=pl.BlockSpec((1, 1, rows, D_MODEL), lambda b, d, i: (d, b, blk(d, i), 0)),
        out_shape=jax.ShapeDtypeStruct((2, bsz, t, D_MODEL), BF16),
        scratch_shapes=[pltpu.VMEM((SSD_STATE, D_MODEL), F32), pltpu.VMEM((SSD_CHUNK, SSD_CHUNK), F32)],
        compiler_params=_params(("parallel", "parallel", "arbitrary")),
        name="ssd_scan",
    )(xbc_act, xbc_act, xbc_act, dt.reshape(2, bsz, t, SSD_HEADS), dtt,
      a_log.reshape(2, 1, SSD_HEADS), a_log.reshape(2, SSD_HEADS, 1))


def _attn_kernel(q_ref, k_ref, v_ref, slope_ref, o_ref, lse_ref, *, sub, dilation, radius, qb, kw):
    def body(qi, carry):
        q0 = pl.multiple_of(qi * qb, qb)
        k0 = pl.multiple_of(jnp.clip(q0 - radius, 0, sub - kw), radius)
        qpos = q0 + lax.broadcasted_iota(jnp.int32, (qb, kw), 0)
        kpos = k0 + lax.broadcasted_iota(jnp.int32, (qb, kw), 1)
        dist = jnp.abs(kpos - qpos)
        valid = dist <= radius
        neg_dist = dist.astype(F32) * (-float(dilation))
        for h in range(ATTN_HPG):
            hs = slice(h * ATTN_HEAD_DIM, (h + 1) * ATTN_HEAD_DIM)
            q = q_ref[0, 0, pl.ds(q0, qb), hs]
            k = k_ref[0, 0, pl.ds(k0, kw), hs]
            v = v_ref[0, 0, pl.ds(k0, kw), hs]
            s = lax.dot_general(q, k, (((1,), (1,)), ((), ())), preferred_element_type=F32)
            s = jnp.where(valid, s + slope_ref[0, h:h + 1, :1] * neg_dist, NEG_INF)
            m = jnp.max(s, axis=-1, keepdims=True)
            p = jnp.exp(s - m)
            l = jnp.sum(p, axis=-1, keepdims=True)
            o = jnp.dot(p.astype(BF16), v, preferred_element_type=F32)
            o_ref[0, 0, pl.ds(q0, qb), hs] = (o / l).astype(o_ref.dtype)
            lse_ref[0, 0, pl.ds(q0, qb), hs] = jnp.broadcast_to(m + jnp.log(l), (qb, LANES))
        return carry

    lax.fori_loop(0, sub // qb, body, 0, unroll=2)


def _attention_group(qkv, slopes, group, window, dilation):
    bsz, _, sub, _ = qkv.shape
    radius = window // (2 * dilation)
    qb = min(2 * radius, sub)
    kw = min(qb + 2 * radius, sub)
    spec = lambda part: pl.BlockSpec((1, 1, sub, ATTN_OUT), lambda b, r: (b, r, 0, part))
    out_spec = pl.BlockSpec((1, 1, sub, ATTN_OUT), lambda b, r: (b, r, 0, 0))
    return pl.pallas_call(
        functools.partial(_attn_kernel, sub=sub, dilation=dilation, radius=radius, qb=qb, kw=kw),
        grid=(bsz, dilation),
        in_specs=[spec(0), spec(1), spec(2),
                  pl.BlockSpec((1, ATTN_HPG, LANES), lambda b, r: (group, 0, 0))],
        out_specs=[out_spec, out_spec],
        out_shape=[jax.ShapeDtypeStruct((bsz, dilation, sub, ATTN_OUT), BF16),
                   jax.ShapeDtypeStruct((bsz, dilation, sub, ATTN_OUT), F32)],
        compiler_params=_params(("parallel", "parallel")),
        name=f"attn_d{dilation}",
    )(qkv, qkv, qkv, slopes)


def _combine_kernel(yf_ref, yb_ref, xs_ref, sz_ref, d_ref, gs_ref,
                    o1_ref, o2_ref, o3_ref, l1_ref, l2_ref, l3_ref, ys_ref, ya_ref,
                    o2_scr, o3_scr, l2_scr, l3_scr):
    y = yf_ref[...].astype(F32) + yb_ref[...].astype(F32) + d_ref[...] * xs_ref[...].astype(F32)
    y = y * sz_ref[...].astype(F32)
    ys_ref[...] = _rms(y, gs_ref[...]).astype(ys_ref.dtype)
    for src_o, src_l, dst_o, dst_l in ((o2_ref, l2_ref, o2_scr, l2_scr), (o3_ref, l3_ref, o3_scr, l3_scr)):
        dil, per = src_o.shape[1], src_o.shape[2]
        for r in range(dil):
            for hh in range(ATTN_HPG):
                hs = slice(hh * LANES, (hh + 1) * LANES)
                dst_o[hh, pl.ds(r, per, stride=dil), :] = src_o[0, r, :, hs].astype(F32)
                dst_l[hh, pl.ds(r, per, stride=dil), :] = src_l[0, r, :, hs]
    for hh in range(ATTN_HPG):
        hs = slice(hh * LANES, (hh + 1) * LANES)
        l1, l2, l3 = l1_ref[0, 0, :, hs], l2_scr[hh], l3_scr[hh]
        mx = jnp.maximum(jnp.maximum(l1, l2), l3)
        w1, w2, w3 = jnp.exp(l1 - mx), jnp.exp(l2 - mx), jnp.exp(l3 - mx)
        num = w1 * o1_ref[0, 0, :, hs].astype(F32) + w2 * o2_scr[hh] + w3 * o3_scr[hh]
        ya_ref[:, hs] = (num / (w1 + w2 + w3)).astype(ya_ref.dtype)


def _combine(y_dirs, xbc_act, zg, d_exp, g_ssd, att, t, tm=512):
    m = zg.shape[0]
    d = D_MODEL
    nt = t // tm
    row = pl.BlockSpec((tm, d), lambda i: (i, 0))
    vec = pl.BlockSpec((1, d), lambda i: (0, 0))
    att_spec = lambda dil: pl.BlockSpec((1, dil, tm // dil, ATTN_OUT), lambda i: (i // nt, 0, i % nt, 0))
    (o1, l1), (o2, l2), (o3, l3) = att
    dils = [dil for _, dil in ATTN_PATTERNS]
    return pl.pallas_call(
        _combine_kernel,
        grid=(m // tm,),
        in_specs=[row, row, row, row, vec, vec,
                  att_spec(dils[0]), att_spec(dils[1]), att_spec(dils[2]),
                  att_spec(dils[0]), att_spec(dils[1]), att_spec(dils[2])],
        out_specs=[row, pl.BlockSpec((tm, ATTN_OUT), lambda i: (i, 0))],
        out_shape=[jax.ShapeDtypeStruct((m, d), BF16), jax.ShapeDtypeStruct((m, ATTN_OUT), BF16)],
        scratch_shapes=[pltpu.VMEM((ATTN_HPG, tm, LANES), F32) for _ in range(4)],
        compiler_params=_params(("parallel",)),
        name="branch_combine",
    )(y_dirs[0], y_dirs[1], xbc_act, zg, d_exp, g_ssd, o1, o2, o3, l1, l2, l3)


def _branch_kernel(ys_ref, ya_ref, yc_ref, g1_ref, g2_ref, g3_ref, ws_ref, wa_ref, wc_ref, out_ref):
    for c in range(out_ref.shape[1] // CHUNK):
        cs = slice(c * CHUNK, (c + 1) * CHUNK)
        acc = g1_ref[:, cs].astype(F32) * jnp.dot(ys_ref[...], ws_ref[:, cs], preferred_element_type=F32)
        acc += g2_ref[:, cs].astype(F32) * jnp.dot(ya_ref[...], wa_ref[:, cs], preferred_element_type=F32)
        acc += g3_ref[:, cs].astype(F32) * jnp.dot(yc_ref[...], wc_ref[:, cs], preferred_element_type=F32)
        out_ref[:, cs] = acc.astype(out_ref.dtype)


def _branch_merge(ys, ya, yc, zg, w_s, w_a, w_c, layer, tm=1024, tn=512):
    m, d = ys.shape
    row = lambda w: pl.BlockSpec((tm, w), lambda i, j: (i, 0))
    gate = lambda g: pl.BlockSpec((tm, tn), lambda i, j: (i, (P_GATE + g * d) // tn + j))
    wspec = lambda k: pl.BlockSpec((None, k, tn), lambda i, j: (layer, 0, j))
    return pl.pallas_call(
        _branch_kernel,
        grid=(m // tm, d // tn),
        in_specs=[row(d), row(ATTN_OUT), row(d), gate(0), gate(1), gate(2),
                  wspec(d), wspec(ATTN_OUT), wspec(d)],
        out_specs=pl.BlockSpec((tm, tn), lambda i, j: (i, j)),
        out_shape=jax.ShapeDtypeStruct((m, d), BF16),
        compiler_params=_params(("parallel", "arbitrary")),
        name="branch_merge",
    )(ys, ya, yc, zg, zg, zg, w_s, w_a, w_c)


def _out_kernel(m_ref, w_ref, x_ref, g_ref, xo_ref, h_ref):
    for c in range(xo_ref.shape[1] // CHUNK):
        cs = slice(c * CHUNK, (c + 1) * CHUNK)
        xo_ref[:, cs] = x_ref[:, cs] + jnp.dot(m_ref[...], w_ref[:, cs], preferred_element_type=F32)
    h_ref[...] = _rms(xo_ref[...], g_ref[...]).astype(h_ref.dtype)


def _out_proj(merged, w_out, layer, x, g2, tm=512):
    m, d = x.shape
    row = pl.BlockSpec((tm, d), lambda i: (i, 0))
    return pl.pallas_call(
        _out_kernel,
        grid=(m // tm,),
        in_specs=[row, pl.BlockSpec((None, d, d), lambda i: (layer, 0, 0)), row,
                  pl.BlockSpec((1, d), lambda i: (0, 0))],
        out_specs=[row, row],
        out_shape=[jax.ShapeDtypeStruct((m, d), F32), jax.ShapeDtypeStruct((m, d), BF16)],
        compiler_params=_params(("parallel",)),
        name="out_proj",
    )(merged, w_out, x, g2.reshape(1, d))


def _mlp_kernel(h_ref, wu_ref, wd_ref, x_ref, g_ref, xo_ref, hn_ref, acc_ref):
    j = pl.program_id(1)

    @pl.when(j == 0)
    def _():
        acc_ref[...] = x_ref[...]

    us = []
    for c in range(wu_ref.shape[1] // CHUNK):
        u = jnp.maximum(jnp.dot(h_ref[...], wu_ref[:, c * CHUNK:(c + 1) * CHUNK], preferred_element_type=F32), 0.0)
        us.append((u * u).astype(BF16))
    u = jnp.concatenate(us, axis=1)
    for c in range(acc_ref.shape[1] // CHUNK):
        cs = slice(c * CHUNK, (c + 1) * CHUNK)
        acc_ref[:, cs] += jnp.dot(u, wd_ref[:, cs], preferred_element_type=F32)

    @pl.when(j == pl.num_programs(1) - 1)
    def _():
        xn = acc_ref[...]
        xo_ref[...] = xn
        hn_ref[...] = _rms(xn, g_ref[...]).astype(hn_ref.dtype)


def _mlp(h2, w_up, w_down, layer, x, g_next, tm=512, th=1024):
    m, d = x.shape
    hid = w_up.shape[2]
    row = pl.BlockSpec((tm, d), lambda i, j: (i, 0))
    return pl.pallas_call(
        _mlp_kernel,
        grid=(m // tm, hid // th),
        in_specs=[row, pl.BlockSpec((None, d, th), lambda i, j: (layer, 0, j)),
                  pl.BlockSpec((None, th, d), lambda i, j: (layer, j, 0)),
                  row, pl.BlockSpec((1, d), lambda i, j: (0, 0))],
        out_specs=[row, row],
        out_shape=[jax.ShapeDtypeStruct((m, d), F32), jax.ShapeDtypeStruct((m, d), BF16)],
        scratch_shapes=[pltpu.VMEM((tm, d), F32)],
        compiler_params=_params(("parallel", "arbitrary")),
        name="mlp",
    )(h2, w_up, w_down, x, g_next.reshape(1, d))


PACK_W = 512


def _pack_kernel(tbl_ref, a_ref, b_ref, dt_in_ref, o_ref, dt_out_ref, *, first_shifted, shift):
    src = tbl_ref[pl.program_id(1)]
    dt_out_ref[...] = dt_in_ref[:, :shift].astype(dt_out_ref.dtype)

    @pl.when(src < first_shifted)
    def _():
        o_ref[...] = a_ref[...].astype(o_ref.dtype)

    @pl.when(src >= first_shifted)
    def _():
        o_ref[:, :PACK_W - shift] = a_ref[:, shift:].astype(o_ref.dtype)
        o_ref[:, PACK_W - shift:] = b_ref[:, :shift].astype(o_ref.dtype)


def _pack_in_proj(w):
    depth, k, _ = w.shape
    shift = COL_Q - COL_DT
    blk = lambda col, width: [col // PACK_W + b for b in range(width // PACK_W)]
    order = blk(COL_Z, D_MODEL) + blk(COL_GATE - shift, N_BRANCHES * D_MODEL) + blk(COL_GLU - shift, 2 * D_MODEL)
    order += blk(COL_XBC, SSD_XBC)
    for g in range(len(ATTN_PATTERNS)):
        for base in (COL_Q, COL_K, COL_V):
            order += blk(base - shift + g * ATTN_OUT, ATTN_OUT)
    table = jnp.asarray(order, jnp.int32)
    sub = PACK_W // LANES
    return pl.pallas_call(
        functools.partial(_pack_kernel, first_shifted=COL_DT // PACK_W, shift=shift),
        grid_spec=pltpu.PrefetchScalarGridSpec(
            num_scalar_prefetch=1,
            grid=(depth, len(order)),
            in_specs=[pl.BlockSpec((None, k, PACK_W), lambda l, j, tbl: (l, 0, tbl[j])),
                      pl.BlockSpec((None, k, LANES), lambda l, j, tbl: (l, 0, (tbl[j] + 1) * sub)),
                      pl.BlockSpec((None, k, LANES), lambda l, j, tbl: (l, 0, COL_DT // LANES))],
            out_specs=[pl.BlockSpec((None, k, PACK_W), lambda l, j, tbl: (l, 0, j)),
                       pl.BlockSpec((None, k, shift), lambda l, j, tbl: (l, 0, 0))]),
        out_shape=[jax.ShapeDtypeStruct((depth, k, len(order) * PACK_W), BF16),
                   jax.ShapeDtypeStruct((depth, k, shift), BF16)],
        compiler_params=_params(("arbitrary", "arbitrary")),
        name="pack_w_in",
    )(table, w, w, w)


def kernel(x, norm1_g, w_in, ssd_conv_w, ssd_conv_b, ssd_dt_bias, ssd_a_log, ssd_d, ssd_norm_g, w_ssd_o,
           q_norm_g, k_norm_g, w_attn_o, conv_dw_w, conv_dw_b, conv_ln_g, conv_ln_b, w_conv_o, w_out,
           norm2_g, w_mlp_up, w_mlp_down):
    bsz, t, d = x.shape
    m = bsz * t
    depth = w_in.shape[0]
    xf = x.reshape(m, d)
    slopes = jnp.exp2(-ALIBI_MAX_EXP * jnp.arange(1, ATTN_HEADS + 1, dtype=F32) / ATTN_HEADS)
    slopes = jnp.broadcast_to(slopes.reshape(len(ATTN_PATTERNS), ATTN_HPG, 1),
                              (len(ATTN_PATTERNS), ATTN_HPG, LANES))
    ones_c = jnp.ones((SSD_XBC,), F32)
    zeros_c = jnp.zeros((SSD_XBC,), F32)
    w, w_dt = _pack_in_proj(w_in)
    w_so, w_ao, w_co, w_o = (v.astype(BF16) for v in (w_ssd_o, w_attn_o, w_conv_o, w_out))
    w_up, w_down = w_mlp_up.astype(BF16), w_mlp_down.astype(BF16)
    h = _norm(xf, norm1_g[0])
    for i in range(depth):
        xbc = _mm(h, w, i, [P_XBC], SSD_XBC, _epi_id, 1024, 1024, "proj_xbc")
        xbc_act = _dwconv(xbc.reshape(bsz, t, SSD_XBC), ssd_conv_w[i], ssd_conv_b[i], ones_c, zeros_c,
                          layer_norm=False)
        glu = _mm(h, w, i, [P_GLU_A, P_GLU_G], D_MODEL, _epi_glu, 1024, 1024, "proj_glu")
        yc = _dwconv(glu.reshape(bsz, t, D_MODEL), conv_dw_w[i], conv_dw_b[i], conv_ln_g[i], conv_ln_b[i],
                     layer_norm=True)
        zg = _mm(h, w, i, [P_Z], P_GLU_A, functools.partial(_epi_z_gates, n_silu_tiles=D_MODEL // 1024),
                 1024, 1024, "proj_z_gates")
        dt, dtt = _dt_proj(h, w_dt[i], ssd_dt_bias[i].reshape(-1))
        qk_gain = jnp.concatenate([jnp.tile(q_norm_g[i] * ATTN_HEAD_DIM ** -0.5, ATTN_HPG),
                                   jnp.tile(k_norm_g[i], ATTN_HPG),
                                   jnp.ones((ATTN_OUT,), F32)]).reshape(1, 3 * ATTN_OUT)
        att = []
        for g, (window, dilation) in enumerate(ATTN_PATTERNS):
            qkv = _qkv_proj(h, w, i, P_QKV + g * 3 * ATTN_OUT, qk_gain, bsz, t, dilation)
            att.append(_attention_group(qkv, slopes, g, window, dilation))
        y_dirs = _ssd_scan(xbc_act, dt, dtt, ssd_a_log[i])

        ys, ya = _combine(y_dirs.reshape(2, m, d), xbc_act.reshape(m, SSD_XBC), zg,
                          jnp.repeat(ssd_d[i], SSD_HEAD_DIM).reshape(1, d), ssd_norm_g[i].reshape(1, d), att, t)
        merged = _branch_merge(ys, ya, yc.reshape(m, d), zg, w_so, w_ao, w_co, i)
        xf, h2 = _out_proj(merged, w_o, i, xf, norm2_g[i])
        g_next = norm1_g[i + 1] if i + 1 < depth else norm1_g[i]
        xf, h = _mlp(h2, w_up, w_down, i, xf, g_next)
    return xf.reshape(bsz, t, d)
```

```python
import functools
import math

import jax
import jax.numpy as jnp
from jax import lax
from jax.experimental import pallas as pl
from jax.experimental.pallas import tpu as pltpu

F32 = jnp.float32
BF16 = jnp.bfloat16

D_MODEL = 2048
SSD_HEAD_DIM = 64
SSD_HEADS = D_MODEL // SSD_HEAD_DIM
SSD_GROUPS = 4
SSD_STATE = 128
SSD_CHUNK = 128
SSD_BC = SSD_GROUPS * SSD_STATE
SSD_XBC = D_MODEL + 2 * SSD_BC
GROUP_W = D_MODEL // SSD_GROUPS
ATTN_HEAD_DIM = 128
ATTN_PATTERNS = ((128, 1), (512, 4), (2048, 16))
ATTN_HPG = 4
ATTN_HEADS = ATTN_HPG * len(ATTN_PATTERNS)
ATTN_WIDTH = ATTN_HEADS * ATTN_HEAD_DIM
ATTN_OUT = ATTN_HPG * ATTN_HEAD_DIM
ALIBI_MAX_EXP = 8.0
NEG_INF = -1e30
CONV_WIDTH = 31
MLP_HIDDEN = 4 * D_MODEL
NORM_EPS = 1e-6
N_BRANCHES = 3

COL_Z = 0
COL_XBC = COL_Z + D_MODEL
COL_DT = COL_XBC + SSD_XBC
COL_Q = COL_DT + 2 * SSD_HEADS
COL_K = COL_Q + ATTN_WIDTH
COL_V = COL_K + ATTN_WIDTH
COL_GLU = COL_V + ATTN_WIDTH
COL_GATE = COL_GLU + 2 * D_MODEL

P_Z = 0
P_GATE = P_Z + D_MODEL
P_GLU_A = P_GATE + N_BRANCHES * D_MODEL
P_GLU_G = P_GLU_A + D_MODEL
P_XBC = P_GLU_G + D_MODEL
P_QKV = P_XBC + SSD_XBC

LANES = 128
HALO = 16
VMEM_LIMIT = 56 * 1024 * 1024
LOG2E = math.log2(math.e)


def _params(sem):
    return pltpu.CompilerParams(dimension_semantics=sem, vmem_limit_bytes=VMEM_LIMIT)


def _rms(x, g):
    return x * lax.rsqrt(jnp.mean(x * x, axis=-1, keepdims=True) + NORM_EPS) * g


def _sigmoid(x):
    return 0.5 * jnp.tanh(0.5 * x) + 0.5


def _norm_kernel(x_ref, g_ref, o_ref):
    o_ref[...] = _rms(x_ref[...], g_ref[...]).astype(o_ref.dtype)


def _norm(x, g, tm=512):
    m, d = x.shape
    return pl.pallas_call(
        _norm_kernel,
        grid=(m // tm,),
        in_specs=[pl.BlockSpec((tm, d), lambda i: (i, 0)), pl.BlockSpec((1, d), lambda i: (0, 0))],
        out_specs=pl.BlockSpec((tm, d), lambda i: (i, 0)),
        out_shape=jax.ShapeDtypeStruct((m, d), BF16),
        compiler_params=_params(("parallel",)),
        name="rmsnorm",
    )(x, g.reshape(1, d))


CHUNK = 256


def _mm_kernel(*refs, nw, epilogue):
    a_ref = refs[0]
    w_refs = refs[1:1 + nw]
    o_ref = refs[-1]
    j = pl.program_id(1)
    for c in range(o_ref.shape[1] // CHUNK):
        cs = slice(c * CHUNK, (c + 1) * CHUNK)
        accs = [jnp.dot(a_ref[...], w[:, cs], preferred_element_type=F32) for w in w_refs]
        o_ref[:, cs] = epilogue(accs, j).astype(o_ref.dtype)


def _mm(a, w, layer, col_starts, n, epilogue, tm, tn, name):
    m, k = a.shape
    in_specs = [pl.BlockSpec((tm, k), lambda i, j: (i, 0))]
    for s in col_starts:
        in_specs.append(pl.BlockSpec((None, k, tn),
                                     functools.partial(lambda i, j, off: (layer, 0, off + j), off=s // tn)))
    return pl.pallas_call(
        functools.partial(_mm_kernel, nw=len(col_starts), epilogue=epilogue),
        grid=(m // tm, n // tn),
        in_specs=in_specs,
        out_specs=pl.BlockSpec((tm, tn), lambda i, j: (i, j)),
        out_shape=jax.ShapeDtypeStruct((m, n), BF16),
        compiler_params=_params(("parallel", "arbitrary")),
        name=name,
    )(a, *([w] * len(col_starts)))


def _epi_id(accs, j):
    return accs[0]


def _epi_glu(accs, j):
    return accs[0] * _sigmoid(accs[1])


def _epi_z_gates(accs, j, *, n_silu_tiles):
    s = _sigmoid(accs[0])
    return jnp.where(j < n_silu_tiles, accs[0] * s, s)


def _qkv_kernel(a_ref, w_ref, gain_ref, o_ref, *scratch, dilation):
    tm = a_ref.shape[0]
    per = tm // dilation
    for c in range(o_ref.shape[3] // CHUNK):
        cs = slice(c * CHUNK, (c + 1) * CHUNK)
        val = jnp.dot(a_ref[...], w_ref[:, cs], preferred_element_type=F32)
        if c * CHUNK < 2 * ATTN_OUT:
            heads = []
            for hh in range(CHUNK // ATTN_HEAD_DIM):
                lo = hh * ATTN_HEAD_DIM
                heads.append(_rms(val[:, lo:lo + ATTN_HEAD_DIM],
                                  gain_ref[:, c * CHUNK + lo:c * CHUNK + lo + ATTN_HEAD_DIM]))
            val = jnp.concatenate(heads, axis=1)
        if dilation == 1:
            o_ref[0, 0, :, cs] = val.astype(o_ref.dtype)
        else:
            scr = scratch[c % len(scratch)]
            for hh in range(CHUNK // LANES):
                scr[hh] = val[:, hh * LANES:(hh + 1) * LANES]
            for r in range(dilation):
                for hh in range(CHUNK // LANES):
                    lo = c * CHUNK + hh * LANES
                    o_ref[0, r, :, lo:lo + LANES] = scr[hh, pl.ds(r, per, stride=dilation), :].astype(o_ref.dtype)


def _qkv_proj(a, w, layer, col_start, gains, bsz, t, dilation, tm=1024):
    m, k = a.shape
    n = 3 * ATTN_OUT
    nt = t // tm
    off = col_start // n
    scratch = [] if dilation == 1 else [pltpu.VMEM((CHUNK // LANES, tm, LANES), F32) for _ in range(2)]
    return pl.pallas_call(
        functools.partial(_qkv_kernel, dilation=dilation),
        grid=(m // tm,),
        in_specs=[pl.BlockSpec((tm, k), lambda i: (i, 0)),
                  pl.BlockSpec((None, k, n), lambda i: (layer, 0, off)),
                  pl.BlockSpec((1, n), lambda i: (0, 0))],
        out_specs=pl.BlockSpec((1, dilation, tm // dilation, n), lambda i: (i // nt, 0, i % nt, 0)),
        out_shape=jax.ShapeDtypeStruct((bsz, dilation, t // dilation, n), BF16),
        scratch_shapes=scratch,
        compiler_params=_params(("parallel",)),
        name=f"proj_qkv_d{dilation}",
    )(a, w, gains)


def _softplus(x):
    return jnp.maximum(x, 0.0) + jnp.log1p(jnp.exp(-jnp.abs(x)))


def _dt_kernel(h_ref, w_ref, wt_ref, b_ref, bt_ref, dt_ref, dtt_ref):
    h = h_ref[...]
    sp = _softplus(jnp.dot(h, w_ref[...], preferred_element_type=F32) + b_ref[...])
    dt_ref[0] = sp[:, :SSD_HEADS]
    dt_ref[1] = sp[:, SSD_HEADS:]
    sp_t = _softplus(lax.dot_general(wt_ref[...], h, (((1,), (1,)), ((), ())),
                                     preferred_element_type=F32) + bt_ref[...])
    for c in range(sp_t.shape[1] // SSD_CHUNK):
        blk = sp_t[:, c * SSD_CHUNK:(c + 1) * SSD_CHUNK]
        dtt_ref[c, 0] = blk[:SSD_HEADS]
        dtt_ref[c, 1] = blk[SSD_HEADS:]


def _dt_proj(h, w_dt_t, bias, tm=512):
    w_dt = w_dt_t.T
    m, k = h.shape
    nd = 2 * SSD_HEADS
    cpb = tm // SSD_CHUNK
    return pl.pallas_call(
        _dt_kernel,
        grid=(m // tm,),
        in_specs=[pl.BlockSpec((tm, k), lambda i: (i, 0)),
                  pl.BlockSpec((k, nd), lambda i: (0, 0)),
                  pl.BlockSpec((nd, k), lambda i: (0, 0)),
                  pl.BlockSpec((1, nd), lambda i: (0, 0)),
                  pl.BlockSpec((nd, 1), lambda i: (0, 0))],
        out_specs=[pl.BlockSpec((2, tm, SSD_HEADS), lambda i: (0, i, 0)),
                   pl.BlockSpec((cpb, 2, SSD_HEADS, SSD_CHUNK), lambda i: (i, 0, 0, 0))],
        out_shape=[jax.ShapeDtypeStruct((2, m, SSD_HEADS), F32),
                   jax.ShapeDtypeStruct((m // SSD_CHUNK, 2, SSD_HEADS, SSD_CHUNK), F32)],
        compiler_params=_params(("parallel",)),
        name="dt_proj",
    )(h, w_dt, w_dt_t, bias.reshape(1, nd), bias.reshape(nd, 1))


def _dwconv_kernel(cur_ref, left_ref, right_ref, w_ref, b_ref, g_ref, beta_ref, o_ref, pad_ref, y_ref,
                   *, taps, tt, layer_norm):
    i = pl.program_id(1)
    n_i = pl.num_programs(1)
    nct = pad_ref.shape[0]
    lmask = (i > 0).astype(F32)
    rmask = (i < n_i - 1).astype(F32)
    for c in range(nct):
        sl = slice(c * LANES, (c + 1) * LANES)
        pad_ref[c, 0:HALO, :] = left_ref[0, :, sl].astype(F32) * lmask
        pad_ref[c, HALO:HALO + tt, :] = cur_ref[0, :, sl].astype(F32)
        pad_ref[c, HALO + tt:, :] = right_ref[0, :, sl].astype(F32) * rmask

    rows = 64
    first = HALO - taps // 2

    def tile_body(c, carry):
        accs = [jnp.broadcast_to(b_ref[c], (rows, LANES)) for _ in range(tt // rows)]
        for k in range(taps):
            wk = w_ref[c, k:k + 1, :]
            for rc in range(tt // rows):
                accs[rc] = accs[rc] + wk * pad_ref[c, pl.ds(first + k + rc * rows, rows), :]
        for rc in range(tt // rows):
            y_ref[c, rc * rows:(rc + 1) * rows, :] = accs[rc]
        return carry

    lax.fori_loop(0, nct, tile_body, 0)

    if layer_norm:
        tot = y_ref[0]
        for c in range(1, nct):
            tot = tot + y_ref[c]
        mu = jnp.sum(tot, axis=-1, keepdims=True) * (1.0 / (nct * LANES))
        sq = jnp.square(y_ref[0] - mu)
        for c in range(1, nct):
            sq = sq + jnp.square(y_ref[c] - mu)
        rstd = lax.rsqrt(jnp.sum(sq, axis=-1, keepdims=True) * (1.0 / (nct * LANES)) + NORM_EPS)
    for c in range(nct):
        sl = slice(c * LANES, (c + 1) * LANES)
        y = y_ref[c]
        if layer_norm:
            y = (y - mu) * rstd * g_ref[c] + beta_ref[c]
        o_ref[0, :, sl] = (y * _sigmoid(y)).astype(o_ref.dtype)


def _dwconv(x, w, b, g, beta, *, layer_norm, tt=256):
    bsz, t, c = x.shape
    taps = w.shape[0]
    nct = c // LANES
    w3 = w.reshape(taps, nct, LANES).transpose(1, 0, 2)
    tiles = lambda v: v.reshape(nct, 1, LANES)
    hb = tt // HALO
    n_halo = t // HALO
    small = lambda shape: pl.BlockSpec(shape, lambda bb, i: (0,) * len(shape))
    return pl.pallas_call(
        functools.partial(_dwconv_kernel, taps=taps, tt=tt, layer_norm=layer_norm),
        grid=(bsz, t // tt),
        in_specs=[pl.BlockSpec((1, tt, c), lambda bb, i: (bb, i, 0)),
                  pl.BlockSpec((1, HALO, c), lambda bb, i: (bb, jnp.maximum(i * hb - 1, 0), 0)),
                  pl.BlockSpec((1, HALO, c), lambda bb, i: (bb, jnp.minimum((i + 1) * hb, n_halo - 1), 0)),
                  small((nct, taps, LANES)), small((nct, 1, LANES)),
                  small((nct, 1, LANES)), small((nct, 1, LANES))],
        out_specs=pl.BlockSpec((1, tt, c), lambda bb, i: (bb, i, 0)),
        out_shape=jax.ShapeDtypeStruct((bsz, t, c), BF16),
        scratch_shapes=[pltpu.VMEM((nct, tt + 2 * HALO, LANES), F32), pltpu.VMEM((nct, tt, LANES), F32)],
        compiler_params=_params(("parallel", "parallel")),
        name="dwconv_ln" if layer_norm else "dwconv",
    )(x, x, x, w3, tiles(b), tiles(g), tiles(beta))


def _head_block_diag(v, lane):
    zero = jnp.zeros_like(v)
    return jnp.concatenate([jnp.where(lane < SSD_HEAD_DIM, v, zero),
                            jnp.where(lane >= SSD_HEAD_DIM, v, zero)], axis=0)


def _split3(v):
    hi = v.astype(BF16)
    r1 = v - hi.astype(F32)
    mid = r1.astype(BF16)
    lo = (r1 - mid.astype(F32)).astype(BF16)
    return jnp.concatenate([hi, mid, lo], axis=1)


def _ssd_kernel(xs_ref, b_ref, c_ref, dt_ref, dtt_ref, alog_ref, alogc_ref, y_ref, state_ref, neg_ref, *, cpb):
    direction = pl.program_id(1)
    fwd = direction == 0
    q = SSD_CHUNK

    @pl.when(pl.program_id(2) == 0)
    def _():
        state_ref[...] = jnp.zeros_like(state_ref)

    row = lax.broadcasted_iota(jnp.int32, (q, q), 0)
    col = lax.broadcasted_iota(jnp.int32, (q, q), 1)
    sign = jnp.where(fwd, 1, -1)
    mask = (row - col) * sign >= 0
    neg_ref[...] = jnp.where(mask, 0.0, -jnp.inf)
    tri = mask.astype(F32)
    tri_t = ((col - row) * sign >= 0).astype(F32)
    head_of_col = lax.broadcasted_iota(jnp.int32, (3 * SSD_HEADS, D_MODEL), 1) // SSD_HEAD_DIM
    head_of_row = lax.broadcasted_iota(jnp.int32, (3 * SSD_HEADS, D_MODEL), 0) % SSD_HEADS
    expand3 = jnp.where(head_of_col == head_of_row, 1.0, 0.0).astype(BF16)
    lane = lax.broadcasted_iota(jnp.int32, (q, LANES), 1)
    a_row = -jnp.exp(alog_ref[0])
    a_col = -jnp.exp(alogc_ref[0])
    hi = lax.Precision.HIGHEST

    for step in range(cpb):
        ci = jnp.where(fwd, step, cpb - 1 - step)
        r0 = pl.multiple_of(ci * q, q)
        x = xs_ref[0, pl.ds(r0, q), :]
        bm = b_ref[0, pl.ds(r0, q), :]
        cm = c_ref[0, pl.ds(r0, q), :]
        dt = dt_ref[0, 0, pl.ds(r0, q), :]
        dtt = dtt_ref[ci, 0]
        cs = jnp.dot(tri, dt * a_row, precision=hi, preferred_element_type=F32)
        cs_t = jnp.dot(dtt * a_col, tri_t, precision=hi, preferred_element_type=F32)
        tot = jnp.where(fwd, cs[q - 1:q, :], cs[0:1, :])
        scale3 = _split3(jnp.concatenate([dt * jnp.exp(tot - cs), jnp.exp(cs)], axis=0))
        cs2 = cs * LOG2E
        cs2_t = cs_t * LOG2E - jnp.log(dtt) * LOG2E
        for g in range(SSD_GROUPS):
            gs = slice(g * SSD_STATE, (g + 1) * SSD_STATE)
            xg = slice(g * GROUP_W, (g + 1) * GROUP_W)
            bg, cg = bm[:, gs], cm[:, gs]
            scale_x = jnp.dot(scale3, expand3[:, xg], preferred_element_type=F32)
            w_state = scale_x[:q]
            e_in = scale_x[q:]
            decay_row = jnp.where(fwd, e_in[q - 1:q, :], e_in[0:1, :])
            xw = (x[:, xg].astype(F32) * w_state).astype(BF16)
            state = state_ref[:, xg]
            cb = lax.dot_general(cg, bg, (((1,), (1,)), ((), ())), preferred_element_type=F32)
            y_off = jnp.dot(cg, state.astype(BF16), preferred_element_type=F32) * e_in
            bg_t = bg.astype(F32).T.astype(BF16)
            state_ref[:, xg] = state * decay_row + jnp.dot(bg_t, xw, preferred_element_type=F32)
            for pr in range(GROUP_W // LANES):
                t = g * (GROUP_W // LANES) + pr
                ts = slice(t * LANES, (t + 1) * LANES)
                ms = []
                for h in (2 * t, 2 * t + 1):
                    seg = (cs2[:, h:h + 1] - cs2_t[h:h + 1, :]) + neg_ref[...]
                    ms.append((cb * jnp.exp2(seg)).astype(BF16))
                y_diag = jnp.dot(jnp.concatenate(ms, axis=1), _head_block_diag(x[:, ts], lane),
                                 preferred_element_type=F32)
                y_ref[0, 0, pl.ds(r0, q), ts] = (y_diag + y_off[:, pr * LANES:(pr + 1) * LANES]).astype(y_ref.dtype)


def _ssd_scan(xbc_act, dt, dtt, a_log, rows=512):
    bsz, t, _ = xbc_act.shape
    nblk = t // rows
    cpb = rows // SSD_CHUNK
    blk = lambda d, i: jnp.where(d == 0, i, nblk - 1 - i)
    bc0 = D_MODEL // SSD_BC
    return pl.pallas_call(
        functools.partial(_ssd_kernel, cpb=cpb),
        grid=(bsz, 2, nblk),
        in_specs=[pl.BlockSpec((1, rows, D_MODEL), lambda b, d, i: (b, blk(d, i), 0)),
                  pl.BlockSpec((1, rows, SSD_BC), lambda b, d, i: (b, blk(d, i), bc0)),
                  pl.BlockSpec((1, rows, SSD_BC), lambda b, d, i: (b, blk(d, i), bc0 + 1)),
                  pl.BlockSpec((1, 1, rows, SSD_HEADS), lambda b, d, i: (d, b, blk(d, i), 0)),
                  pl.BlockSpec((cpb, 1, SSD_HEADS, SSD_CHUNK),
                               lambda b, d, i: (b * nblk + blk(d, i), d, 0, 0)),
                  pl.BlockSpec((1, 1, SSD_HEADS), lambda b, d, i: (d, 0, 0)),
                  pl.BlockSpec((1, SSD_HEADS, 1), lambda b, d, i: (d, 0, 0))],
        out_specs=pl.BlockSpec((1, 1, rows, D_MODEL), lambda b, d, i: (d, b, blk(d, i), 0)),
        out_shape=jax.ShapeDtypeStruct((2, bsz, t, D_MODEL), BF16),
        scratch_shapes=[pltpu.VMEM((SSD_STATE, D_MODEL), F32), pltpu.VMEM((SSD_CHUNK, SSD_CHUNK), F32)],
        compiler_params=_params(("parallel", "parallel", "arbitrary")),
        name="ssd_scan",
    )(xbc_act, xbc_act, xbc_act, dt.reshape(2, bsz, t, SSD_HEADS), dtt,
      a_log.reshape(2, 1, SSD_HEADS), a_log.reshape(2, SSD_HEADS, 1))


def _attn_kernel(q_ref, k_ref, v_ref, slope_ref, o_ref, lse_ref, *, sub, dilation, radius, qb, kw):
    def body(qi, carry):
        q0 = pl.multiple_of(qi * qb, qb)
        k0 = pl.multiple_of(jnp.clip(q0 - radius, 0, sub - kw), radius)
        qpos = q0 + lax.broadcasted_iota(jnp.int32, (qb, kw), 0)
        kpos = k0 + lax.broadcasted_iota(jnp.int32, (qb, kw), 1)
        dist = jnp.abs(kpos - qpos)
        valid = dist <= radius
        neg_dist = dist.astype(F32) * (-float(dilation))
        for h in range(ATTN_HPG):
            hs = slice(h * ATTN_HEAD_DIM, (h + 1) * ATTN_HEAD_DIM)
            q = q_ref[0, 0, pl.ds(q0, qb), hs]
            k = k_ref[0, 0, pl.ds(k0, kw), hs]
            v = v_ref[0, 0, pl.ds(k0, kw), hs]
            s = lax.dot_general(q, k, (((1,), (1,)), ((), ())), preferred_element_type=F32)
            s = jnp.where(valid, s + slope_ref[0, h:h + 1, :1] * neg_dist, NEG_INF)
            m = jnp.max(s, axis=-1, keepdims=True)
            p = jnp.exp(s - m)
            l = jnp.sum(p, axis=-1, keepdims=True)
            o = jnp.dot(p.astype(BF16), v, preferred_element_type=F32)
            o_ref[0, 0, pl.ds(q0, qb), hs] = (o / l).astype(o_ref.dtype)
            lse_ref[0, 0, pl.ds(q0, qb), hs] = jnp.broadcast_to(m + jnp.log(l), (qb, LANES))
        return carry

    lax.fori_loop(0, sub // qb, body, 0, unroll=2)


def _attention_group(qkv, slopes, group, window, dilation):
    bsz, _, sub, _ = qkv.shape
    radius = window // (2 * dilation)
    qb = min(2 * radius, sub)
    kw = min(qb + 2 * radius, sub)
    spec = lambda part: pl.BlockSpec((1, 1, sub, ATTN_OUT), lambda b, r: (b, r, 0, part))
    out_spec = pl.BlockSpec((1, 1, sub, ATTN_OUT), lambda b, r: (b, r, 0, 0))
    return pl.pallas_call(
        functools.partial(_attn_kernel, sub=sub, dilation=dilation, radius=radius, qb=qb, kw=kw),
        grid=(bsz, dilation),
        in_specs=[spec(0), spec(1), spec(2),
                  pl.BlockSpec((1, ATTN_HPG, LANES), lambda b, r: (group, 0, 0))],
        out_specs=[out_spec, out_spec],
        out_shape=[jax.ShapeDtypeStruct((bsz, dilation, sub, ATTN_OUT), BF16),
                   jax.ShapeDtypeStruct((bsz, dilation, sub, ATTN_OUT), F32)],
        compiler_params=_params(("parallel", "parallel")),
        name=f"attn_d{dilation}",
    )(qkv, qkv, qkv, slopes)


def _combine_kernel(yf_ref, yb_ref, xs_ref, sz_ref, d_ref, gs_ref,
                    o1_ref, o2_ref, o3_ref, l1_ref, l2_ref, l3_ref, ys_ref, ya_ref,
                    o2_scr, o3_scr, l2_scr, l3_scr):
    y = yf_ref[...].astype(F32) + yb_ref[...].astype(F32) + d_ref[...] * xs_ref[...].astype(F32)
    y = y * sz_ref[...].astype(F32)
    ys_ref[...] = _rms(y, gs_ref[...]).astype(ys_ref.dtype)
    for src_o, src_l, dst_o, dst_l in ((o2_ref, l2_ref, o2_scr, l2_scr), (o3_ref, l3_ref, o3_scr, l3_scr)):
        dil, per = src_o.shape[1], src_o.shape[2]
        for r in range(dil):
            for hh in range(ATTN_HPG):
                hs = slice(hh * LANES, (hh + 1) * LANES)
                dst_o[hh, pl.ds(r, per, stride=dil), :] = src_o[0, r, :, hs].astype(F32)
                dst_l[hh, pl.ds(r, per, stride=dil), :] = src_l[0, r, :, hs]
    for hh in range(ATTN_HPG):
        hs = slice(hh * LANES, (hh + 1) * LANES)
        l1, l2, l3 = l1_ref[0, 0, :, hs], l2_scr[hh], l3_scr[hh]
        mx = jnp.maximum(jnp.maximum(l1, l2), l3)
        w1, w2, w3 = jnp.exp(l1 - mx), jnp.exp(l2 - mx), jnp.exp(l3 - mx)
        num = w1 * o1_ref[0, 0, :, hs].astype(F32) + w2 * o2_scr[hh] + w3 * o3_scr[hh]
        ya_ref[:, hs] = (num / (w1 + w2 + w3)).astype(ya_ref.dtype)


def _combine(y_dirs, xbc_act, zg, d_exp, g_ssd, att, t, tm=512):
    m = zg.shape[0]
    d = D_MODEL
    nt = t // tm
    row = pl.BlockSpec((tm, d), lambda i: (i, 0))
    direction = lambda dd: pl.BlockSpec((None, tm, d), lambda i: (dd, i, 0))
    vec = pl.BlockSpec((1, d), lambda i: (0, 0))
    att_spec = lambda dil: pl.BlockSpec((1, dil, tm // dil, ATTN_OUT), lambda i: (i // nt, 0, i % nt, 0))
    (o1, l1), (o2, l2), (o3, l3) = att
    dils = [dil for _, dil in ATTN_PATTERNS]
    return pl.pallas_call(
        _combine_kernel,
        grid=(m // tm,),
        in_specs=[direction(0), direction(1), row, row, vec, vec,
                  att_spec(dils[0]), att_spec(dils[1]), att_spec(dils[2]),
                  att_spec(dils[0]), att_spec(dils[1]), att_spec(dils[2])],
        out_specs=[row, pl.BlockSpec((tm, ATTN_OUT), lambda i: (i, 0))],
        out_shape=[jax.ShapeDtypeStruct((m, d), BF16), jax.ShapeDtypeStruct((m, ATTN_OUT), BF16)],
        scratch_shapes=[pltpu.VMEM((ATTN_HPG, tm, LANES), F32) for _ in range(4)],
        compiler_params=_params(("parallel",)),
        name="branch_combine",
    )(y_dirs, y_dirs, xbc_act, zg, d_exp, g_ssd, o1, o2, o3, l1, l2, l3)


def _branch_kernel(ys_ref, ya_ref, yc_ref, g1_ref, g2_ref, g3_ref, ws_ref, wa_ref, wc_ref, out_ref):
    for c in range(out_ref.shape[1] // CHUNK):
        cs = slice(c * CHUNK, (c + 1) * CHUNK)
        acc = g1_ref[:, cs].astype(F32) * jnp.dot(ys_ref[...], ws_ref[:, cs], preferred_element_type=F32)
        acc += g2_ref[:, cs].astype(F32) * jnp.dot(ya_ref[...], wa_ref[:, cs], preferred_element_type=F32)
        acc += g3_ref[:, cs].astype(F32) * jnp.dot(yc_ref[...], wc_ref[:, cs], preferred_element_type=F32)
        out_ref[:, cs] = acc.astype(out_ref.dtype)


def _branch_merge(ys, ya, yc, zg, w_s, w_a, w_c, layer, tm=1024, tn=512):
    m, d = ys.shape
    row = lambda w: pl.BlockSpec((tm, w), lambda i, j: (i, 0))
    gate = lambda g: pl.BlockSpec((tm, tn), lambda i, j: (i, (P_GATE + g * d) // tn + j))
    wspec = lambda k: pl.BlockSpec((None, k, tn), lambda i, j: (layer, 0, j))
    return pl.pallas_call(
        _branch_kernel,
        grid=(m // tm, d // tn),
        in_specs=[row(d), row(ATTN_OUT), row(d), gate(0), gate(1), gate(2),
                  wspec(d), wspec(ATTN_OUT), wspec(d)],
        out_specs=pl.BlockSpec((tm, tn), lambda i, j: (i, j)),
        out_shape=jax.ShapeDtypeStruct((m, d), BF16),
        compiler_params=_params(("parallel", "arbitrary")),
        name="branch_merge",
    )(ys, ya, yc, zg, zg, zg, w_s, w_a, w_c)


def _out_kernel(m_ref, w_ref, x_ref, g_ref, xo_ref, h_ref):
    for c in range(xo_ref.shape[1] // CHUNK):
        cs = slice(c * CHUNK, (c + 1) * CHUNK)
        xo_ref[:, cs] = x_ref[:, cs] + jnp.dot(m_ref[...], w_ref[:, cs], preferred_element_type=F32)
    h_ref[...] = _rms(xo_ref[...], g_ref[...]).astype(h_ref.dtype)


def _out_proj(merged, w_out, layer, x, g2, tm=512):
    m, d = x.shape
    row = pl.BlockSpec((tm, d), lambda i: (i, 0))
    return pl.pallas_call(
        _out_kernel,
        grid=(m // tm,),
        in_specs=[row, pl.BlockSpec((None, d, d), lambda i: (layer, 0, 0)), row,
                  pl.BlockSpec((1, d), lambda i: (0, 0))],
        out_specs=[row, row],
        out_shape=[jax.ShapeDtypeStruct((m, d), F32), jax.ShapeDtypeStruct((m, d), BF16)],
        compiler_params=_params(("parallel",)),
        name="out_proj",
    )(merged, w_out, x, g2.reshape(1, d))


def _mlp_kernel(h_ref, wu_ref, wd_ref, x_ref, g_ref, xo_ref, *maybe_hn_ref):
    j = pl.program_id(1)

    @pl.when(j == 0)
    def _():
        xo_ref[...] = x_ref[...]

    us = []
    for c in range(wu_ref.shape[1] // CHUNK):
        u = jnp.maximum(jnp.dot(h_ref[...], wu_ref[:, c * CHUNK:(c + 1) * CHUNK], preferred_element_type=F32), 0.0)
        us.append((u * u).astype(BF16))
    u = jnp.concatenate(us, axis=1)
    for c in range(xo_ref.shape[1] // CHUNK):
        cs = slice(c * CHUNK, (c + 1) * CHUNK)
        xo_ref[:, cs] += jnp.dot(u, wd_ref[:, cs], preferred_element_type=F32)

    if maybe_hn_ref:
        @pl.when(j == pl.num_programs(1) - 1)
        def _():
            maybe_hn_ref[0][...] = _rms(xo_ref[...], g_ref[...]).astype(maybe_hn_ref[0].dtype)


def _mlp(h2, w_up, w_down, layer, x, g_next, tm=512, th=1024):
    m, d = x.shape
    hid = w_up.shape[2]
    row = pl.BlockSpec((tm, d), lambda i, j: (i, 0))
    with_norm = g_next is not None
    gain = (g_next if with_norm else jnp.ones((d,), F32)).reshape(1, d)
    res = pl.pallas_call(
        _mlp_kernel,
        grid=(m // tm, hid // th),
        in_specs=[row, pl.BlockSpec((None, d, th), lambda i, j: (layer, 0, j)),
                  pl.BlockSpec((None, th, d), lambda i, j: (layer, j, 0)),
                  row, pl.BlockSpec((1, d), lambda i, j: (0, 0))],
        out_specs=[row, row] if with_norm else [row],
        out_shape=[jax.ShapeDtypeStruct((m, d), F32)] + ([jax.ShapeDtypeStruct((m, d), BF16)] if with_norm else []),
        compiler_params=_params(("parallel", "arbitrary")),
        name="mlp",
    )(h2, w_up, w_down, x, gain)
    return res if with_norm else (res[0], None)


PACK_W = 512


def _pack_kernel(tbl_ref, a_ref, b_ref, dt_in_ref, o_ref, dt_out_ref, *, first_shifted, shift):
    src = tbl_ref[pl.program_id(1)]
    dt_out_ref[...] = dt_in_ref[...].astype(dt_out_ref.dtype)

    @pl.when(src < first_shifted)
    def _():
        o_ref[...] = a_ref[...].T.astype(o_ref.dtype)

    @pl.when(src >= first_shifted)
    def _():
        rows = jnp.concatenate([a_ref[shift:, :], b_ref[...]], axis=0)
        o_ref[...] = rows.T.astype(o_ref.dtype)


def _pack_in_proj(w_t):
    depth, _, k = w_t.shape
    shift = COL_Q - COL_DT
    blk = lambda col, width: [col // PACK_W + b for b in range(width // PACK_W)]
    order = blk(COL_Z, D_MODEL) + blk(COL_GATE - shift, N_BRANCHES * D_MODEL) + blk(COL_GLU - shift, 2 * D_MODEL)
    order += blk(COL_XBC, SSD_XBC)
    for g in range(len(ATTN_PATTERNS)):
        for base in (COL_Q, COL_K, COL_V):
            order += blk(base - shift + g * ATTN_OUT, ATTN_OUT)
    table = jnp.asarray(order, jnp.int32)
    sub = PACK_W // shift
    return pl.pallas_call(
        functools.partial(_pack_kernel, first_shifted=COL_DT // PACK_W, shift=shift),
        grid_spec=pltpu.PrefetchScalarGridSpec(
            num_scalar_prefetch=1,
            grid=(depth, len(order)),
            in_specs=[pl.BlockSpec((None, PACK_W, k), lambda l, j, tbl: (l, tbl[j], 0)),
                      pl.BlockSpec((None, shift, k), lambda l, j, tbl: (l, (tbl[j] + 1) * sub, 0)),
                      pl.BlockSpec((None, shift, k), lambda l, j, tbl: (l, COL_DT // shift, 0))],
            out_specs=[pl.BlockSpec((None, k, PACK_W), lambda l, j, tbl: (l, 0, j)),
                       pl.BlockSpec((None, shift, k), lambda l, j, tbl: (l, 0, 0))]),
        out_shape=[jax.ShapeDtypeStruct((depth, k, len(order) * PACK_W), BF16),
                   jax.ShapeDtypeStruct((depth, shift, k), BF16)],
        compiler_params=_params(("arbitrary", "arbitrary")),
        name="pack_w_in",
    )(table, w_t, w_t, w_t)


def kernel(x, norm1_g, w_in, ssd_conv_w, ssd_conv_b, ssd_dt_bias, ssd_a_log, ssd_d, ssd_norm_g, w_ssd_o,
           q_norm_g, k_norm_g, w_attn_o, conv_dw_w, conv_dw_b, conv_ln_g, conv_ln_b, w_conv_o, w_out,
           norm2_g, w_mlp_up, w_mlp_down):
    bsz, t, d = x.shape
    m = bsz * t
    depth = w_in.shape[0]
    xf = x.reshape(m, d)
    slopes = jnp.exp2(-ALIBI_MAX_EXP * jnp.arange(1, ATTN_HEADS + 1, dtype=F32) / ATTN_HEADS)
    slopes = jnp.broadcast_to(slopes.reshape(len(ATTN_PATTERNS), ATTN_HPG, 1),
                              (len(ATTN_PATTERNS), ATTN_HPG, LANES))
    ones_c = jnp.ones((SSD_XBC,), F32)
    zeros_c = jnp.zeros((SSD_XBC,), F32)
    w, w_dt_t = _pack_in_proj(jnp.swapaxes(w_in, 1, 2))
    w_so, w_ao, w_co, w_o = (v.astype(BF16) for v in (w_ssd_o, w_attn_o, w_conv_o, w_out))
    w_up, w_down = w_mlp_up.astype(BF16), w_mlp_down.astype(BF16)
    h = _norm(xf, norm1_g[0])
    for i in range(depth):
        xbc = _mm(h, w, i, [P_XBC], SSD_XBC, _epi_id, 1024, 1024, "proj_xbc")
        xbc_act = _dwconv(xbc.reshape(bsz, t, SSD_XBC), ssd_conv_w[i], ssd_conv_b[i], ones_c, zeros_c,
                          layer_norm=False)
        glu = _mm(h, w, i, [P_GLU_A, P_GLU_G], D_MODEL, _epi_glu, 1024, 1024, "proj_glu")
        yc = _dwconv(glu.reshape(bsz, t, D_MODEL), conv_dw_w[i], conv_dw_b[i], conv_ln_g[i], conv_ln_b[i],
                     layer_norm=True)
        zg = _mm(h, w, i, [P_Z], P_GLU_A, functools.partial(_epi_z_gates, n_silu_tiles=D_MODEL // 1024),
                 1024, 1024, "proj_z_gates")
        dt, dtt = _dt_proj(h, w_dt_t[i], ssd_dt_bias[i].reshape(-1))
        qk_gain = jnp.concatenate([jnp.tile(q_norm_g[i] * ATTN_HEAD_DIM ** -0.5, ATTN_HPG),
                                   jnp.tile(k_norm_g[i], ATTN_HPG),
                                   jnp.ones((ATTN_OUT,), F32)]).reshape(1, 3 * ATTN_OUT)
        att = []
        for g, (window, dilation) in enumerate(ATTN_PATTERNS):
            qkv = _qkv_proj(h, w, i, P_QKV + g * 3 * ATTN_OUT, qk_gain, bsz, t, dilation)
            att.append(_attention_group(qkv, slopes, g, window, dilation))
        y_dirs = _ssd_scan(xbc_act, dt, dtt, ssd_a_log[i])

        ys, ya = _combine(y_dirs.reshape(2, m, d), xbc_act.reshape(m, SSD_XBC), zg,
                          jnp.repeat(ssd_d[i], SSD_HEAD_DIM).reshape(1, d), ssd_norm_g[i].reshape(1, d), att, t)
        merged = _branch_merge(ys, ya, yc.reshape(m, d), zg, w_so, w_ao, w_co, i)
        xf, h2 = _out_proj(merged, w_o, i, xf, norm2_g[i])
        xf, h = _mlp(h2, w_up, w_down, i, xf, norm1_g[i + 1] if i + 1 < depth else None)
    return xf.reshape(bsz, t, d)
```

```python
import functools
import math

import jax
import jax.numpy as jnp
from jax import lax
from jax.experimental import pallas as pl
from jax.experimental.pallas import tpu as pltpu

F32 = jnp.float32
BF16 = jnp.bfloat16

D_MODEL = 2048
SSD_HEAD_DIM = 64
SSD_HEADS = D_MODEL // SSD_HEAD_DIM
SSD_GROUPS = 4
SSD_STATE = 128
SSD_CHUNK = 128
SSD_BC = SSD_GROUPS * SSD_STATE
SSD_XBC = D_MODEL + 2 * SSD_BC
GROUP_W = D_MODEL // SSD_GROUPS
ATTN_HEAD_DIM = 128
ATTN_PATTERNS = ((128, 1), (512, 4), (2048, 16))
ATTN_HPG = 4
ATTN_HEADS = ATTN_HPG * len(ATTN_PATTERNS)
ATTN_WIDTH = ATTN_HEADS * ATTN_HEAD_DIM
ATTN_OUT = ATTN_HPG * ATTN_HEAD_DIM
ALIBI_MAX_EXP = 8.0
NEG_INF = -1e30
CONV_WIDTH = 31
MLP_HIDDEN = 4 * D_MODEL
NORM_EPS = 1e-6
N_BRANCHES = 3

COL_Z = 0
COL_XBC = COL_Z + D_MODEL
COL_DT = COL_XBC + SSD_XBC
COL_Q = COL_DT + 2 * SSD_HEADS
COL_K = COL_Q + ATTN_WIDTH
COL_V = COL_K + ATTN_WIDTH
COL_GLU = COL_V + ATTN_WIDTH
COL_GATE = COL_GLU + 2 * D_MODEL

P_Z = 0
P_GATE = P_Z + D_MODEL
P_GLU_A = P_GATE + N_BRANCHES * D_MODEL
P_GLU_G = P_GLU_A + D_MODEL
P_XBC = P_GLU_G + D_MODEL
P_QKV = P_XBC + SSD_XBC

LANES = 128
HALO = 16
VMEM_LIMIT = 56 * 1024 * 1024
LOG2E = math.log2(math.e)


def _params(sem):
    return pltpu.CompilerParams(dimension_semantics=sem, vmem_limit_bytes=VMEM_LIMIT)


def _rms(x, g):
    return x * lax.rsqrt(jnp.mean(x * x, axis=-1, keepdims=True) + NORM_EPS) * g


def _sigmoid(x):
    return 0.5 * jnp.tanh(0.5 * x) + 0.5


def _norm_kernel(x_ref, g_ref, o_ref):
    o_ref[...] = _rms(x_ref[...], g_ref[...]).astype(o_ref.dtype)


def _norm(x, g, tm=512):
    m, d = x.shape
    return pl.pallas_call(
        _norm_kernel,
        grid=(m // tm,),
        in_specs=[pl.BlockSpec((tm, d), lambda i: (i, 0)), pl.BlockSpec((1, d), lambda i: (0, 0))],
        out_specs=pl.BlockSpec((tm, d), lambda i: (i, 0)),
        out_shape=jax.ShapeDtypeStruct((m, d), BF16),
        compiler_params=_params(("parallel",)),
        name="rmsnorm",
    )(x, g.reshape(1, d))


CHUNK = 256


def _mm_kernel(*refs, nw, epilogue):
    a_ref = refs[0]
    w_refs = refs[1:1 + nw]
    o_ref = refs[-1]
    j = pl.program_id(1)
    for c in range(o_ref.shape[1] // CHUNK):
        cs = slice(c * CHUNK, (c + 1) * CHUNK)
        accs = [jnp.dot(a_ref[...], w[:, cs], preferred_element_type=F32) for w in w_refs]
        o_ref[:, cs] = epilogue(accs, j).astype(o_ref.dtype)


def _mm(a, w, layer, col_starts, n, epilogue, tm, tn, name):
    m, k = a.shape
    in_specs = [pl.BlockSpec((tm, k), lambda i, j: (i, 0))]
    for s in col_starts:
        in_specs.append(pl.BlockSpec((None, k, tn),
                                     functools.partial(lambda i, j, off: (layer, 0, off + j), off=s // tn)))
    return pl.pallas_call(
        functools.partial(_mm_kernel, nw=len(col_starts), epilogue=epilogue),
        grid=(m // tm, n // tn),
        in_specs=in_specs,
        out_specs=pl.BlockSpec((tm, tn), lambda i, j: (i, j)),
        out_shape=jax.ShapeDtypeStruct((m, n), BF16),
        compiler_params=_params(("parallel", "arbitrary")),
        name=name,
    )(a, *([w] * len(col_starts)))


def _epi_glu(accs, j):
    return accs[0] * _sigmoid(accs[1])


def _epi_z_gates(accs, j, *, n_silu_tiles):
    a = accs[0].astype(BF16)
    s = _sigmoid(a)
    return jnp.where(j < n_silu_tiles, a * s, s)


def _qkv_kernel(a_ref, w_ref, gain_ref, o_ref, *scratch, dilation):
    tm = a_ref.shape[0]
    per = tm // dilation
    for c in range(o_ref.shape[3] // CHUNK):
        cs = slice(c * CHUNK, (c + 1) * CHUNK)
        val = jnp.dot(a_ref[...], w_ref[:, cs], preferred_element_type=F32)
        if c * CHUNK < 2 * ATTN_OUT:
            heads = []
            for hh in range(CHUNK // ATTN_HEAD_DIM):
                lo = hh * ATTN_HEAD_DIM
                heads.append(_rms(val[:, lo:lo + ATTN_HEAD_DIM],
                                  gain_ref[:, c * CHUNK + lo:c * CHUNK + lo + ATTN_HEAD_DIM]))
            val = jnp.concatenate(heads, axis=1)
        if dilation == 1:
            o_ref[0, 0, :, cs] = val.astype(o_ref.dtype)
        else:
            scr = scratch[c % len(scratch)]
            for hh in range(CHUNK // LANES):
                scr[hh] = val[:, hh * LANES:(hh + 1) * LANES]
            for r in range(dilation):
                for hh in range(CHUNK // LANES):
                    lo = c * CHUNK + hh * LANES
                    o_ref[0, r, :, lo:lo + LANES] = scr[hh, pl.ds(r, per, stride=dilation), :].astype(o_ref.dtype)


def _qkv_proj(a, w, layer, col_start, gains, bsz, t, dilation, tm=1024):
    m, k = a.shape
    n = 3 * ATTN_OUT
    nt = t // tm
    off = col_start // n
    scratch = [] if dilation == 1 else [pltpu.VMEM((CHUNK // LANES, tm, LANES), F32) for _ in range(2)]
    return pl.pallas_call(
        functools.partial(_qkv_kernel, dilation=dilation),
        grid=(m // tm,),
        in_specs=[pl.BlockSpec((tm, k), lambda i: (i, 0)),
                  pl.BlockSpec((None, k, n), lambda i: (layer, 0, off)),
                  pl.BlockSpec((1, n), lambda i: (0, 0))],
        out_specs=pl.BlockSpec((1, dilation, tm // dilation, n), lambda i: (i // nt, 0, i % nt, 0)),
        out_shape=jax.ShapeDtypeStruct((bsz, dilation, t // dilation, n), BF16),
        scratch_shapes=scratch,
        compiler_params=_params(("parallel",)),
        name=f"proj_qkv_d{dilation}",
    )(a, w, gains)


def _softplus(x):
    return jnp.maximum(x, 0.0) + jnp.log1p(jnp.exp(-jnp.abs(x)))


def _xbc_dt_kernel(h_ref, w_ref, wdt_ref, wdtt_ref, b_ref, bt_ref, xbc_ref, dt_ref, dtt_ref):
    for c in range(xbc_ref.shape[1] // CHUNK):
        cs = slice(c * CHUNK, (c + 1) * CHUNK)
        xbc_ref[:, cs] = jnp.dot(h_ref[...], w_ref[:, cs], preferred_element_type=F32).astype(xbc_ref.dtype)

    @pl.when(pl.program_id(1) == 0)
    def _():
        h = h_ref[...]
        sp = _softplus(jnp.dot(h, wdt_ref[...], preferred_element_type=F32) + b_ref[...])
        dt_ref[0] = sp[:, :SSD_HEADS]
        dt_ref[1] = sp[:, SSD_HEADS:]
        sp_t = _softplus(lax.dot_general(wdtt_ref[...], h, (((1,), (1,)), ((), ())),
                                         preferred_element_type=F32) + bt_ref[...])
        for c in range(sp_t.shape[1] // SSD_CHUNK):
            blk = sp_t[:, c * SSD_CHUNK:(c + 1) * SSD_CHUNK]
            dtt_ref[c, 0] = blk[:SSD_HEADS]
            dtt_ref[c, 1] = blk[SSD_HEADS:]


def _xbc_dt_proj(h, w, layer, w_dt_t, bias, tm=1024, tn=1024):
    m, k = h.shape
    nd = 2 * SSD_HEADS
    cpb = tm // SSD_CHUNK
    const = lambda shape: pl.BlockSpec(shape, lambda i, j: (0, 0))
    return pl.pallas_call(
        _xbc_dt_kernel,
        grid=(m // tm, SSD_XBC // tn),
        in_specs=[pl.BlockSpec((tm, k), lambda i, j: (i, 0)),
                  pl.BlockSpec((None, k, tn), lambda i, j: (layer, 0, P_XBC // tn + j)),
                  const((k, nd)), const((nd, k)), const((1, nd)), const((nd, 1))],
        out_specs=[pl.BlockSpec((tm, tn), lambda i, j: (i, j)),
                   pl.BlockSpec((2, tm, SSD_HEADS), lambda i, j: (0, i, 0)),
                   pl.BlockSpec((cpb, 2, SSD_HEADS, SSD_CHUNK), lambda i, j: (i, 0, 0, 0))],
        out_shape=[jax.ShapeDtypeStruct((m, SSD_XBC), BF16),
                   jax.ShapeDtypeStruct((2, m, SSD_HEADS), F32),
                   jax.ShapeDtypeStruct((m // SSD_CHUNK, 2, SSD_HEADS, SSD_CHUNK), F32)],
        compiler_params=_params(("parallel", "arbitrary")),
        name="proj_xbc_dt",
    )(h, w, w_dt_t.T, w_dt_t, bias.reshape(1, nd), bias.reshape(nd, 1))


def _dwconv_kernel(cur_ref, left_ref, right_ref, w_ref, b_ref, g_ref, beta_ref, o_ref, pad_ref, y_ref,
                   *, taps, tt, layer_norm):
    i = pl.program_id(1)
    n_i = pl.num_programs(1)
    nct = pad_ref.shape[0]
    lmask = (i > 0).astype(F32)
    rmask = (i < n_i - 1).astype(F32)
    for c in range(nct):
        sl = slice(c * LANES, (c + 1) * LANES)
        pad_ref[c, 0:HALO, :] = left_ref[0, :, sl].astype(F32) * lmask
        pad_ref[c, HALO:HALO + tt, :] = cur_ref[0, :, sl].astype(F32)
        pad_ref[c, HALO + tt:, :] = right_ref[0, :, sl].astype(F32) * rmask

    rows = 64
    first = HALO - taps // 2

    def tile_body(c, carry):
        accs = [jnp.broadcast_to(b_ref[c], (rows, LANES)) for _ in range(tt // rows)]
        for k in range(taps):
            wk = w_ref[c, k:k + 1, :]
            for rc in range(tt // rows):
                accs[rc] = accs[rc] + wk * pad_ref[c, pl.ds(first + k + rc * rows, rows), :]
        for rc in range(tt // rows):
            y_ref[c, rc * rows:(rc + 1) * rows, :] = accs[rc]
        return carry

    lax.fori_loop(0, nct, tile_body, 0)

    if layer_norm:
        tot = y_ref[0]
        for c in range(1, nct):
            tot = tot + y_ref[c]
        mu = jnp.sum(tot, axis=-1, keepdims=True) * (1.0 / (nct * LANES))
        sq = jnp.square(y_ref[0] - mu)
        for c in range(1, nct):
            sq = sq + jnp.square(y_ref[c] - mu)
        rstd = lax.rsqrt(jnp.sum(sq, axis=-1, keepdims=True) * (1.0 / (nct * LANES)) + NORM_EPS)
    for c in range(nct):
        sl = slice(c * LANES, (c + 1) * LANES)
        y = y_ref[c]
        if layer_norm:
            y = (y - mu) * rstd * g_ref[c] + beta_ref[c]
        o_ref[0, :, sl] = (y * _sigmoid(y)).astype(o_ref.dtype)


def _dwconv(x, w, b, g, beta, *, layer_norm, tt=256):
    bsz, t, c = x.shape
    taps = w.shape[0]
    nct = c // LANES
    w3 = w.reshape(taps, nct, LANES).transpose(1, 0, 2)
    tiles = lambda v: v.reshape(nct, 1, LANES)
    hb = tt // HALO
    n_halo = t // HALO
    small = lambda shape: pl.BlockSpec(shape, lambda bb, i: (0,) * len(shape))
    return pl.pallas_call(
        functools.partial(_dwconv_kernel, taps=taps, tt=tt, layer_norm=layer_norm),
        grid=(bsz, t // tt),
        in_specs=[pl.BlockSpec((1, tt, c), lambda bb, i: (bb, i, 0)),
                  pl.BlockSpec((1, HALO, c), lambda bb, i: (bb, jnp.maximum(i * hb - 1, 0), 0)),
                  pl.BlockSpec((1, HALO, c), lambda bb, i: (bb, jnp.minimum((i + 1) * hb, n_halo - 1), 0)),
                  small((nct, taps, LANES)), small((nct, 1, LANES)),
                  small((nct, 1, LANES)), small((nct, 1, LANES))],
        out_specs=pl.BlockSpec((1, tt, c), lambda bb, i: (bb, i, 0)),
        out_shape=jax.ShapeDtypeStruct((bsz, t, c), BF16),
        scratch_shapes=[pltpu.VMEM((nct, tt + 2 * HALO, LANES), F32), pltpu.VMEM((nct, tt, LANES), F32)],
        compiler_params=_params(("parallel", "parallel")),
        name="dwconv_ln" if layer_norm else "dwconv",
    )(x, x, x, w3, tiles(b), tiles(g), tiles(beta))


def _head_block_diag(v, lane):
    zero = jnp.zeros_like(v)
    return jnp.concatenate([jnp.where(lane < SSD_HEAD_DIM, v, zero),
                            jnp.where(lane >= SSD_HEAD_DIM, v, zero)], axis=0)


def _split3(v):
    hi = v.astype(BF16)
    r1 = v - hi.astype(F32)
    mid = r1.astype(BF16)
    lo = (r1 - mid.astype(F32)).astype(BF16)
    return jnp.concatenate([hi, mid, lo], axis=1)


def _ssd_kernel(xs_ref, b_ref, c_ref, dt_ref, dtt_ref, alog_ref, alogc_ref, y_ref, state_ref, neg_ref, *, cpb):
    direction = pl.program_id(1)
    fwd = direction == 0
    q = SSD_CHUNK

    @pl.when(pl.program_id(2) == 0)
    def _():
        state_ref[...] = jnp.zeros_like(state_ref)

    row = lax.broadcasted_iota(jnp.int32, (q, q), 0)
    col = lax.broadcasted_iota(jnp.int32, (q, q), 1)
    sign = jnp.where(fwd, 1, -1)
    mask = (row - col) * sign >= 0
    neg_ref[...] = jnp.where(mask, 0.0, -jnp.inf)
    tri = mask.astype(F32)
    tri_t = ((col - row) * sign >= 0).astype(F32)
    head_of_col = lax.broadcasted_iota(jnp.int32, (3 * SSD_HEADS, D_MODEL), 1) // SSD_HEAD_DIM
    head_of_row = lax.broadcasted_iota(jnp.int32, (3 * SSD_HEADS, D_MODEL), 0) % SSD_HEADS
    expand3 = jnp.where(head_of_col == head_of_row, 1.0, 0.0).astype(BF16)
    lane = lax.broadcasted_iota(jnp.int32, (q, LANES), 1)
    a_row = -jnp.exp(alog_ref[0])
    a_col = -jnp.exp(alogc_ref[0])
    hi = lax.Precision.HIGHEST

    for step in range(cpb):
        ci = jnp.where(fwd, step, cpb - 1 - step)
        r0 = pl.multiple_of(ci * q, q)
        x = xs_ref[0, pl.ds(r0, q), :]
        bm = b_ref[0, pl.ds(r0, q), :]
        cm = c_ref[0, pl.ds(r0, q), :]
        dt = dt_ref[0, 0, pl.ds(r0, q), :]
        dtt = dtt_ref[ci, 0]
        cs = jnp.dot(tri, dt * a_row, precision=hi, preferred_element_type=F32)
        cs_t = jnp.dot(dtt * a_col, tri_t, precision=hi, preferred_element_type=F32)
        tot = jnp.where(fwd, cs[q - 1:q, :], cs[0:1, :])
        scale3 = _split3(jnp.concatenate([dt * jnp.exp(tot - cs), jnp.exp(cs)], axis=0))
        cs2 = cs * LOG2E
        cs2_t = cs_t * LOG2E - jnp.log(dtt) * LOG2E
        for g in range(SSD_GROUPS):
            gs = slice(g * SSD_STATE, (g + 1) * SSD_STATE)
            xg = slice(g * GROUP_W, (g + 1) * GROUP_W)
            bg, cg = bm[:, gs], cm[:, gs]
            scale_x = jnp.dot(scale3, expand3[:, xg], preferred_element_type=F32)
            w_state = scale_x[:q]
            e_in = scale_x[q:]
            decay_row = jnp.where(fwd, e_in[q - 1:q, :], e_in[0:1, :])
            xw = (x[:, xg].astype(F32) * w_state).astype(BF16)
            state = state_ref[:, xg]
            cb = lax.dot_general(cg, bg, (((1,), (1,)), ((), ())), preferred_element_type=F32)
            y_off = jnp.dot(cg, state.astype(BF16), preferred_element_type=F32) * e_in
            bg_t = bg.astype(F32).T.astype(BF16)
            state_ref[:, xg] = state * decay_row + jnp.dot(bg_t, xw, preferred_element_type=F32)
            for pr in range(GROUP_W // LANES):
                t = g * (GROUP_W // LANES) + pr
                ts = slice(t * LANES, (t + 1) * LANES)
                ms = []
                for h in (2 * t, 2 * t + 1):
                    seg = (cs2[:, h:h + 1] - cs2_t[h:h + 1, :]) + neg_ref[...]
                    ms.append((cb * jnp.exp2(seg)).astype(BF16))
                y_diag = jnp.dot(jnp.concatenate(ms, axis=1), _head_block_diag(x[:, ts], lane),
                                 preferred_element_type=F32)
                y_ref[0, 0, pl.ds(r0, q), ts] = (y_diag + y_off[:, pr * LANES:(pr + 1) * LANES]).astype(y_ref.dtype)


def _ssd_scan(xbc_act, dt, dtt, a_log, rows=512):
    bsz, t, _ = xbc_act.shape
    nblk = t // rows
    cpb = rows // SSD_CHUNK
    blk = lambda d, i: jnp.where(d == 0, i, nblk - 1 - i)
    bc0 = D_MODEL // SSD_BC
    return pl.pallas_call(
        functools.partial(_ssd_kernel, cpb=cpb),
        grid=(bsz, 2, nblk),
        in_specs=[pl.BlockSpec((1, rows, D_MODEL), lambda b, d, i: (b, blk(d, i), 0)),
                  pl.BlockSpec((1, rows, SSD_BC), lambda b, d, i: (b, blk(d, i), bc0)),
                  pl.BlockSpec((1, rows, SSD_BC), lambda b, d, i: (b, blk(d, i), bc0 + 1)),
                  pl.BlockSpec((1, 1, rows, SSD_HEADS), lambda b, d, i: (d, b, blk(d, i), 0)),
                  pl.BlockSpec((cpb, 1, SSD_HEADS, SSD_CHUNK),
                               lambda b, d, i: (b * nblk + blk(d, i), d, 0, 0)),
                  pl.BlockSpec((1, 1, SSD_HEADS), lambda b, d, i: (d, 0, 0)),
                  pl.BlockSpec((1, SSD_HEADS, 1), lambda b, d, i: (d, 0, 0))],
        out_specs=pl.BlockSpec((1, 1, rows, D_MODEL), lambda b, d, i: (d, b, blk(d, i), 0)),
        out_shape=jax.ShapeDtypeStruct((2, bsz, t, D_MODEL), BF16),
        scratch_shapes=[pltpu.VMEM((SSD_STATE, D_MODEL), F32), pltpu.VMEM((SSD_CHUNK, SSD_CHUNK), F32)],
        compiler_params=_params(("parallel", "parallel", "arbitrary")),
        name="ssd_scan",
    )(xbc_act, xbc_act, xbc_act, dt.reshape(2, bsz, t, SSD_HEADS), dtt,
      a_log.reshape(2, 1, SSD_HEADS), a_log.reshape(2, SSD_HEADS, 1))


def _attn_kernel(q_ref, k_ref, v_ref, slope_ref, o_ref, lse_ref, *, sub, dilation, radius, qb, kw):
    def body(qi, carry):
        q0 = pl.multiple_of(qi * qb, qb)
        k0 = pl.multiple_of(jnp.clip(q0 - radius, 0, sub - kw), radius)
        qpos = q0 + lax.broadcasted_iota(jnp.int32, (qb, kw), 0)
        kpos = k0 + lax.broadcasted_iota(jnp.int32, (qb, kw), 1)
        dist = jnp.abs(kpos - qpos)
        valid = dist <= radius
        neg_dist = dist.astype(F32) * (-float(dilation))
        for h in range(ATTN_HPG):
            hs = slice(h * ATTN_HEAD_DIM, (h + 1) * ATTN_HEAD_DIM)
            q = q_ref[0, 0, pl.ds(q0, qb), hs]
            k = k_ref[0, 0, pl.ds(k0, kw), hs]
            v = v_ref[0, 0, pl.ds(k0, kw), hs]
            s = lax.dot_general(q, k, (((1,), (1,)), ((), ())), preferred_element_type=F32)
            s = jnp.where(valid, s + slope_ref[0, h:h + 1, :1] * neg_dist, NEG_INF)
            m = jnp.max(s, axis=-1, keepdims=True)
            p = jnp.exp(s - m)
            l = jnp.sum(p, axis=-1, keepdims=True)
            o = jnp.dot(p.astype(BF16), v, preferred_element_type=F32)
            o_ref[0, 0, pl.ds(q0, qb), hs] = (o / l).astype(o_ref.dtype)
            lse_ref[0, 0, pl.ds(q0, qb), hs] = jnp.broadcast_to(m + jnp.log(l), (qb, LANES))
        return carry

    lax.fori_loop(0, sub // qb, body, 0, unroll=min(4, sub // qb))


def _attention_group(qkv, slopes, group, window, dilation):
    bsz, _, sub, _ = qkv.shape
    radius = window // (2 * dilation)
    qb = min(2 * radius, sub)
    kw = min(qb + 2 * radius, sub)
    spec = lambda part: pl.BlockSpec((1, 1, sub, ATTN_OUT), lambda b, r: (b, r, 0, part))
    out_spec = pl.BlockSpec((1, 1, sub, ATTN_OUT), lambda b, r: (b, r, 0, 0))
    return pl.pallas_call(
        functools.partial(_attn_kernel, sub=sub, dilation=dilation, radius=radius, qb=qb, kw=kw),
        grid=(bsz, dilation),
        in_specs=[spec(0), spec(1), spec(2),
                  pl.BlockSpec((1, ATTN_HPG, LANES), lambda b, r: (group, 0, 0))],
        out_specs=[out_spec, out_spec],
        out_shape=[jax.ShapeDtypeStruct((bsz, dilation, sub, ATTN_OUT), BF16),
                   jax.ShapeDtypeStruct((bsz, dilation, sub, ATTN_OUT), F32)],
        compiler_params=_params(("parallel", "parallel")),
        name=f"attn_d{dilation}",
    )(qkv, qkv, qkv, slopes)


def _combine_kernel(yf_ref, yb_ref, xs_ref, sz_ref, d_ref, gs_ref,
                    o1_ref, o2_ref, o3_ref, l1_ref, l2_ref, l3_ref, ys_ref, ya_ref,
                    o2_scr, o3_scr, l2_scr, l3_scr):
    y = yf_ref[...].astype(F32) + yb_ref[...].astype(F32) + d_ref[...] * xs_ref[...].astype(F32)
    y = y * sz_ref[...].astype(F32)
    ys_ref[...] = _rms(y, gs_ref[...]).astype(ys_ref.dtype)
    for src_o, src_l, dst_o, dst_l in ((o2_ref, l2_ref, o2_scr, l2_scr), (o3_ref, l3_ref, o3_scr, l3_scr)):
        dil, per = src_o.shape[1], src_o.shape[2]
        for r in range(dil):
            for hh in range(ATTN_HPG):
                hs = slice(hh * LANES, (hh + 1) * LANES)
                dst_o[hh, pl.ds(r, per, stride=dil), :] = src_o[0, r, :, hs].astype(F32)
                dst_l[hh, pl.ds(r, per, stride=dil), :] = src_l[0, r, :, hs]
    for hh in range(ATTN_HPG):
        hs = slice(hh * LANES, (hh + 1) * LANES)
        l1, l2, l3 = l1_ref[0, 0, :, hs], l2_scr[hh], l3_scr[hh]
        mx = jnp.maximum(jnp.maximum(l1, l2), l3)
        w1, w2, w3 = jnp.exp(l1 - mx), jnp.exp(l2 - mx), jnp.exp(l3 - mx)
        num = w1 * o1_ref[0, 0, :, hs].astype(F32) + w2 * o2_scr[hh] + w3 * o3_scr[hh]
        ya_ref[:, hs] = (num / (w1 + w2 + w3)).astype(ya_ref.dtype)


def _combine(y_dirs, xbc_act, zg, d_exp, g_ssd, att, t, tm=512):
    m = zg.shape[0]
    d = D_MODEL
    nt = t // tm
    row = pl.BlockSpec((tm, d), lambda i: (i, 0))
    direction = lambda dd: pl.BlockSpec((None, tm, d), lambda i: (dd, i, 0))
    vec = pl.BlockSpec((1, d), lambda i: (0, 0))
    att_spec = lambda dil: pl.BlockSpec((1, dil, tm // dil, ATTN_OUT), lambda i: (i // nt, 0, i % nt, 0))
    (o1, l1), (o2, l2), (o3, l3) = att
    dils = [dil for _, dil in ATTN_PATTERNS]
    return pl.pallas_call(
        _combine_kernel,
        grid=(m // tm,),
        in_specs=[direction(0), direction(1), row, row, vec, vec,
                  att_spec(dils[0]), att_spec(dils[1]), att_spec(dils[2]),
                  att_spec(dils[0]), att_spec(dils[1]), att_spec(dils[2])],
        out_specs=[row, pl.BlockSpec((tm, ATTN_OUT), lambda i: (i, 0))],
        out_shape=[jax.ShapeDtypeStruct((m, d), BF16), jax.ShapeDtypeStruct((m, ATTN_OUT), BF16)],
        scratch_shapes=[pltpu.VMEM((ATTN_HPG, tm, LANES), F32) for _ in range(4)],
        compiler_params=_params(("parallel",)),
        name="branch_combine",
    )(y_dirs, y_dirs, xbc_act, zg, d_exp, g_ssd, o1, o2, o3, l1, l2, l3)


def _branch_kernel(ys_ref, ya_ref, yc_ref, g1_ref, g2_ref, g3_ref, ws_ref, wa_ref, wc_ref, out_ref):
    for c in range(out_ref.shape[1] // CHUNK):
        cs = slice(c * CHUNK, (c + 1) * CHUNK)
        acc = g1_ref[:, cs].astype(F32) * jnp.dot(ys_ref[...], ws_ref[:, cs], preferred_element_type=F32)
        acc += g2_ref[:, cs].astype(F32) * jnp.dot(ya_ref[...], wa_ref[:, cs], preferred_element_type=F32)
        acc += g3_ref[:, cs].astype(F32) * jnp.dot(yc_ref[...], wc_ref[:, cs], preferred_element_type=F32)
        out_ref[:, cs] = acc.astype(out_ref.dtype)


def _branch_merge(ys, ya, yc, zg, w_s, w_a, w_c, layer, tm=1024, tn=512):
    m, d = ys.shape
    row = lambda w: pl.BlockSpec((tm, w), lambda i, j: (i, 0))
    gate = lambda g: pl.BlockSpec((tm, tn), lambda i, j: (i, (P_GATE + g * d) // tn + j))
    wspec = lambda k: pl.BlockSpec((None, k, tn), lambda i, j: (layer, 0, j))
    return pl.pallas_call(
        _branch_kernel,
        grid=(m // tm, d // tn),
        in_specs=[row(d), row(ATTN_OUT), row(d), gate(0), gate(1), gate(2),
                  wspec(d), wspec(ATTN_OUT), wspec(d)],
        out_specs=pl.BlockSpec((tm, tn), lambda i, j: (i, j)),
        out_shape=jax.ShapeDtypeStruct((m, d), BF16),
        compiler_params=_params(("parallel", "arbitrary")),
        name="branch_merge",
    )(ys, ya, yc, zg, zg, zg, w_s, w_a, w_c)


def _out_kernel(m_ref, w_ref, x_ref, g_ref, xo_ref, h_ref):
    for c in range(xo_ref.shape[1] // CHUNK):
        cs = slice(c * CHUNK, (c + 1) * CHUNK)
        xo_ref[:, cs] = x_ref[:, cs] + jnp.dot(m_ref[...], w_ref[:, cs], preferred_element_type=F32)
    h_ref[...] = _rms(xo_ref[...], g_ref[...]).astype(h_ref.dtype)


def _out_proj(merged, w_out, layer, x, g2, tm=512):
    m, d = x.shape
    row = pl.BlockSpec((tm, d), lambda i: (i, 0))
    return pl.pallas_call(
        _out_kernel,
        grid=(m // tm,),
        in_specs=[row, pl.BlockSpec((None, d, d), lambda i: (layer, 0, 0)), row,
                  pl.BlockSpec((1, d), lambda i: (0, 0))],
        out_specs=[row, row],
        out_shape=[jax.ShapeDtypeStruct((m, d), F32), jax.ShapeDtypeStruct((m, d), BF16)],
        compiler_params=_params(("parallel",)),
        name="out_proj",
    )(merged, w_out, x, g2.reshape(1, d))


def _mlp_kernel(h_ref, wu_ref, wd_ref, x_ref, g_ref, xo_ref, *maybe_hn_ref):
    j = pl.program_id(1)

    @pl.when(j == 0)
    def _():
        xo_ref[...] = x_ref[...]

    us = []
    for c in range(wu_ref.shape[1] // CHUNK):
        u = jnp.maximum(jnp.dot(h_ref[...], wu_ref[:, c * CHUNK:(c + 1) * CHUNK], preferred_element_type=F32), 0.0)
        us.append((u * u).astype(BF16))
    u = jnp.concatenate(us, axis=1)
    for c in range(xo_ref.shape[1] // CHUNK):
        cs = slice(c * CHUNK, (c + 1) * CHUNK)
        xo_ref[:, cs] += jnp.dot(u, wd_ref[:, cs], preferred_element_type=F32)

    if maybe_hn_ref:
        @pl.when(j == pl.num_programs(1) - 1)
        def _():
            maybe_hn_ref[0][...] = _rms(xo_ref[...], g_ref[...]).astype(maybe_hn_ref[0].dtype)


def _mlp(h2, w_up, w_down, layer, x, g_next, tm=512, th=1024):
    m, d = x.shape
    hid = w_up.shape[2]
    row = pl.BlockSpec((tm, d), lambda i, j: (i, 0))
    with_norm = g_next is not None
    gain = (g_next if with_norm else jnp.ones((d,), F32)).reshape(1, d)
    res = pl.pallas_call(
        _mlp_kernel,
        grid=(m // tm, hid // th),
        in_specs=[row, pl.BlockSpec((None, d, th), lambda i, j: (layer, 0, j)),
                  pl.BlockSpec((None, th, d), lambda i, j: (layer, j, 0)),
                  row, pl.BlockSpec((1, d), lambda i, j: (0, 0))],
        out_specs=[row, row] if with_norm else [row],
        out_shape=[jax.ShapeDtypeStruct((m, d), F32)] + ([jax.ShapeDtypeStruct((m, d), BF16)] if with_norm else []),
        compiler_params=_params(("parallel", "arbitrary")),
        name="mlp",
    )(h2, w_up, w_down, x, gain)
    return res if with_norm else (res[0], None)


PACK_W = 512


def _pack_kernel(tbl_ref, a_ref, b_ref, dt_in_ref, o_ref, dt_out_ref, *, first_shifted, shift):
    src = tbl_ref[pl.program_id(1)]
    dt_out_ref[...] = dt_in_ref[...].astype(dt_out_ref.dtype)

    @pl.when(src < first_shifted)
    def _():
        o_ref[...] = a_ref[...].T.astype(o_ref.dtype)

    @pl.when(src >= first_shifted)
    def _():
        rows = jnp.concatenate([a_ref[shift:, :], b_ref[...]], axis=0)
        o_ref[...] = rows.T.astype(o_ref.dtype)


def _pack_in_proj(w_t):
    depth, _, k = w_t.shape
    shift = COL_Q - COL_DT
    blk = lambda col, width: [col // PACK_W + b for b in range(width // PACK_W)]
    order = blk(COL_Z, D_MODEL) + blk(COL_GATE - shift, N_BRANCHES * D_MODEL) + blk(COL_GLU - shift, 2 * D_MODEL)
    order += blk(COL_XBC, SSD_XBC)
    for g in range(len(ATTN_PATTERNS)):
        for base in (COL_Q, COL_K, COL_V):
            order += blk(base - shift + g * ATTN_OUT, ATTN_OUT)
    table = jnp.asarray(order, jnp.int32)
    sub = PACK_W // shift
    return pl.pallas_call(
        functools.partial(_pack_kernel, first_shifted=COL_DT // PACK_W, shift=shift),
        grid_spec=pltpu.PrefetchScalarGridSpec(
            num_scalar_prefetch=1,
            grid=(depth, len(order)),
            in_specs=[pl.BlockSpec((None, PACK_W, k), lambda l, j, tbl: (l, tbl[j], 0)),
                      pl.BlockSpec((None, shift, k), lambda l, j, tbl: (l, (tbl[j] + 1) * sub, 0)),
                      pl.BlockSpec((None, shift, k), lambda l, j, tbl: (l, COL_DT // shift, 0))],
            out_specs=[pl.BlockSpec((None, k, PACK_W), lambda l, j, tbl: (l, 0, j)),
                       pl.BlockSpec((None, shift, k), lambda l, j, tbl: (l, 0, 0))]),
        out_shape=[jax.ShapeDtypeStruct((depth, k, len(order) * PACK_W), BF16),
                   jax.ShapeDtypeStruct((depth, shift, k), BF16)],
        compiler_params=_params(("arbitrary", "arbitrary")),
        name="pack_w_in",
    )(table, w_t, w_t, w_t)


def kernel(x, norm1_g, w_in, ssd_conv_w, ssd_conv_b, ssd_dt_bias, ssd_a_log, ssd_d, ssd_norm_g, w_ssd_o,
           q_norm_g, k_norm_g, w_attn_o, conv_dw_w, conv_dw_b, conv_ln_g, conv_ln_b, w_conv_o, w_out,
           norm2_g, w_mlp_up, w_mlp_down):
    bsz, t, d = x.shape
    m = bsz * t
    depth = w_in.shape[0]
    xf = x.reshape(m, d)
    slopes = jnp.exp2(-ALIBI_MAX_EXP * jnp.arange(1, ATTN_HEADS + 1, dtype=F32) / ATTN_HEADS)
    slopes = jnp.broadcast_to(slopes.reshape(len(ATTN_PATTERNS), ATTN_HPG, 1),
                              (len(ATTN_PATTERNS), ATTN_HPG, LANES))
    ones_c = jnp.ones((SSD_XBC,), F32)
    zeros_c = jnp.zeros((SSD_XBC,), F32)
    w, w_dt_t = _pack_in_proj(jnp.swapaxes(w_in, 1, 2))
    w_so, w_ao, w_co, w_o = (v.astype(BF16) for v in (w_ssd_o, w_attn_o, w_conv_o, w_out))
    w_up, w_down = w_mlp_up.astype(BF16), w_mlp_down.astype(BF16)
    h = _norm(xf, norm1_g[0])
    for i in range(depth):
        xbc, dt, dtt = _xbc_dt_proj(h, w, i, w_dt_t[i], ssd_dt_bias[i].reshape(-1))
        xbc_act = _dwconv(xbc.reshape(bsz, t, SSD_XBC), ssd_conv_w[i], ssd_conv_b[i], ones_c, zeros_c,
                          layer_norm=False)
        glu = _mm(h, w, i, [P_GLU_A, P_GLU_G], D_MODEL, _epi_glu, 1024, 1024, "proj_glu")
        yc = _dwconv(glu.reshape(bsz, t, D_MODEL), conv_dw_w[i], conv_dw_b[i], conv_ln_g[i], conv_ln_b[i],
                     layer_norm=True)
        zg = _mm(h, w, i, [P_Z], P_GLU_A, functools.partial(_epi_z_gates, n_silu_tiles=D_MODEL // 1024),
                 1024, 1024, "proj_z_gates")
        qk_gain = jnp.concatenate([jnp.tile(q_norm_g[i] * ATTN_HEAD_DIM ** -0.5, ATTN_HPG),
                                   jnp.tile(k_norm_g[i], ATTN_HPG),
                                   jnp.ones((ATTN_OUT,), F32)]).reshape(1, 3 * ATTN_OUT)
        att = []
        for g, (window, dilation) in enumerate(ATTN_PATTERNS):
            qkv = _qkv_proj(h, w, i, P_QKV + g * 3 * ATTN_OUT, qk_gain, bsz, t, dilation)
            att.append(_attention_group(qkv, slopes, g, window, dilation))
        y_dirs = _ssd_scan(xbc_act, dt, dtt, ssd_a_log[i])

        ys, ya = _combine(y_dirs.reshape(2, m, d), xbc_act.reshape(m, SSD_XBC), zg,
                          jnp.repeat(ssd_d[i], SSD_HEAD_DIM).reshape(1, d), ssd_norm_g[i].reshape(1, d), att, t)
        merged = _branch_merge(ys, ya, yc.reshape(m, d), zg, w_so, w_ao, w_co, i)
        xf, h2 = _out_proj(merged, w_o, i, xf, norm2_g[i])
        xf, h = _mlp(h2, w_up, w_down, i, xf, norm1_g[i + 1] if i + 1 < depth else None)
    return xf.reshape(bsz, t, d)
```

```python
import functools
import math

import jax
import jax.numpy as jnp
from jax import lax
from jax.experimental import pallas as pl
from jax.experimental.pallas import tpu as pltpu

F32 = jnp.float32
BF16 = jnp.bfloat16

D_MODEL = 2048
SSD_HEAD_DIM = 64
SSD_HEADS = D_MODEL // SSD_HEAD_DIM
SSD_GROUPS = 4
SSD_STATE = 128
SSD_CHUNK = 128
SSD_BC = SSD_GROUPS * SSD_STATE
SSD_XBC = D_MODEL + 2 * SSD_BC
GROUP_W = D_MODEL // SSD_GROUPS
ATTN_HEAD_DIM = 128
ATTN_PATTERNS = ((128, 1), (512, 4), (2048, 16))
ATTN_HPG = 4
ATTN_HEADS = ATTN_HPG * len(ATTN_PATTERNS)
ATTN_WIDTH = ATTN_HEADS * ATTN_HEAD_DIM
ATTN_OUT = ATTN_HPG * ATTN_HEAD_DIM
ALIBI_MAX_EXP = 8.0
NEG_INF = -1e30
CONV_WIDTH = 31
MLP_HIDDEN = 4 * D_MODEL
NORM_EPS = 1e-6
N_BRANCHES = 3

COL_Z = 0
COL_XBC = COL_Z + D_MODEL
COL_DT = COL_XBC + SSD_XBC
COL_Q = COL_DT + 2 * SSD_HEADS
COL_K = COL_Q + ATTN_WIDTH
COL_V = COL_K + ATTN_WIDTH
COL_GLU = COL_V + ATTN_WIDTH
COL_GATE = COL_GLU + 2 * D_MODEL

P_Z = 0
P_GATE = P_Z + D_MODEL
P_GLU_A = P_GATE + N_BRANCHES * D_MODEL
P_GLU_G = P_GLU_A + D_MODEL
P_XBC = P_GLU_G + D_MODEL
P_QKV = P_XBC + SSD_XBC

LANES = 128
HALO = 16
VMEM_LIMIT = 56 * 1024 * 1024
LOG2E = math.log2(math.e)


def _params(sem):
    return pltpu.CompilerParams(dimension_semantics=sem, vmem_limit_bytes=VMEM_LIMIT)


def _rms(x, g):
    return x * lax.rsqrt(jnp.mean(x * x, axis=-1, keepdims=True) + NORM_EPS) * g


def _sigmoid(x):
    return 0.5 * jnp.tanh(0.5 * x) + 0.5


def _norm_kernel(x_ref, g_ref, o_ref):
    o_ref[...] = _rms(x_ref[...], g_ref[...]).astype(o_ref.dtype)


def _norm(x, g, tm=512):
    m, d = x.shape
    return pl.pallas_call(
        _norm_kernel,
        grid=(m // tm,),
        in_specs=[pl.BlockSpec((tm, d), lambda i: (i, 0)), pl.BlockSpec((1, d), lambda i: (0, 0))],
        out_specs=pl.BlockSpec((tm, d), lambda i: (i, 0)),
        out_shape=jax.ShapeDtypeStruct((m, d), BF16),
        compiler_params=_params(("parallel",)),
        name="rmsnorm",
    )(x, g.reshape(1, d))


CHUNK = 256


def _mm_kernel(*refs, nw, epilogue):
    a_ref = refs[0]
    w_refs = refs[1:1 + nw]
    o_ref = refs[-1]
    j = pl.program_id(1)
    for c in range(o_ref.shape[1] // CHUNK):
        cs = slice(c * CHUNK, (c + 1) * CHUNK)
        accs = [jnp.dot(a_ref[...], w[:, cs], preferred_element_type=F32) for w in w_refs]
        o_ref[:, cs] = epilogue(accs, j).astype(o_ref.dtype)


def _mm(a, w, layer, col_starts, n, epilogue, tm, tn, name):
    m, k = a.shape
    in_specs = [pl.BlockSpec((tm, k), lambda i, j: (i, 0))]
    for s in col_starts:
        in_specs.append(pl.BlockSpec((None, k, tn),
                                     functools.partial(lambda i, j, off: (layer, 0, off + j), off=s // tn)))
    return pl.pallas_call(
        functools.partial(_mm_kernel, nw=len(col_starts), epilogue=epilogue),
        grid=(m // tm, n // tn),
        in_specs=in_specs,
        out_specs=pl.BlockSpec((tm, tn), lambda i, j: (i, j)),
        out_shape=jax.ShapeDtypeStruct((m, n), BF16),
        compiler_params=_params(("parallel", "arbitrary")),
        name=name,
    )(a, *([w] * len(col_starts)))


def _epi_glu(accs, j):
    return accs[0] * _sigmoid(accs[1])


def _epi_z_gates(accs, j, *, n_silu_tiles):
    a = accs[0].astype(BF16)
    s = _sigmoid(a)
    return jnp.where(j < n_silu_tiles, a * s, s)


def _qkv_kernel(a_ref, w_ref, gain_ref, o_ref, *scratch, dilation):
    tm = a_ref.shape[0]
    per = tm // dilation
    for c in range(o_ref.shape[3] // CHUNK):
        cs = slice(c * CHUNK, (c + 1) * CHUNK)
        val = jnp.dot(a_ref[...], w_ref[:, cs], preferred_element_type=F32)
        if c * CHUNK < 2 * ATTN_OUT:
            heads = []
            for hh in range(CHUNK // ATTN_HEAD_DIM):
                lo = hh * ATTN_HEAD_DIM
                heads.append(_rms(val[:, lo:lo + ATTN_HEAD_DIM],
                                  gain_ref[:, c * CHUNK + lo:c * CHUNK + lo + ATTN_HEAD_DIM]))
            val = jnp.concatenate(heads, axis=1)
        if dilation == 1:
            o_ref[0, 0, :, cs] = val.astype(o_ref.dtype)
        else:
            scr = scratch[c % len(scratch)]
            for hh in range(CHUNK // LANES):
                scr[hh] = val[:, hh * LANES:(hh + 1) * LANES]
            for r in range(dilation):
                for hh in range(CHUNK // LANES):
                    lo = c * CHUNK + hh * LANES
                    o_ref[0, r, :, lo:lo + LANES] = scr[hh, pl.ds(r, per, stride=dilation), :].astype(o_ref.dtype)


def _qkv_proj(a, w, layer, col_start, gains, bsz, t, dilation, tm=1024):
    m, k = a.shape
    n = 3 * ATTN_OUT
    nt = t // tm
    off = col_start // n
    scratch = [] if dilation == 1 else [pltpu.VMEM((CHUNK // LANES, tm, LANES), F32) for _ in range(2)]
    return pl.pallas_call(
        functools.partial(_qkv_kernel, dilation=dilation),
        grid=(m // tm,),
        in_specs=[pl.BlockSpec((tm, k), lambda i: (i, 0)),
                  pl.BlockSpec((None, k, n), lambda i: (layer, 0, off)),
                  pl.BlockSpec((1, n), lambda i: (0, 0))],
        out_specs=pl.BlockSpec((1, dilation, tm // dilation, n), lambda i: (i // nt, 0, i % nt, 0)),
        out_shape=jax.ShapeDtypeStruct((bsz, dilation, t // dilation, n), BF16),
        scratch_shapes=scratch,
        compiler_params=_params(("parallel",)),
        name=f"proj_qkv_d{dilation}",
    )(a, w, gains)


def _softplus(x):
    return jnp.maximum(x, 0.0) + jnp.log1p(jnp.exp(-jnp.abs(x)))


def _xbc_dt_kernel(h_ref, w_ref, wdt_ref, wdtt_ref, b_ref, bt_ref, xbc_ref, dt_ref, dtt_ref):
    for c in range(xbc_ref.shape[1] // CHUNK):
        cs = slice(c * CHUNK, (c + 1) * CHUNK)
        xbc_ref[:, cs] = jnp.dot(h_ref[...], w_ref[:, cs], preferred_element_type=F32).astype(xbc_ref.dtype)

    @pl.when(pl.program_id(1) == 0)
    def _():
        h = h_ref[...]
        sp = _softplus(jnp.dot(h, wdt_ref[...], preferred_element_type=F32) + b_ref[...])
        dt_ref[0] = sp[:, :SSD_HEADS]
        dt_ref[1] = sp[:, SSD_HEADS:]
        sp_t = _softplus(lax.dot_general(wdtt_ref[...], h, (((1,), (1,)), ((), ())),
                                         preferred_element_type=F32) + bt_ref[...])
        for c in range(sp_t.shape[1] // SSD_CHUNK):
            blk = sp_t[:, c * SSD_CHUNK:(c + 1) * SSD_CHUNK]
            dtt_ref[c, 0] = blk[:SSD_HEADS]
            dtt_ref[c, 1] = blk[SSD_HEADS:]


def _xbc_dt_proj(h, w, layer, w_dt_t, bias, tm=1024, tn=1024):
    m, k = h.shape
    nd = 2 * SSD_HEADS
    cpb = tm // SSD_CHUNK
    const = lambda shape: pl.BlockSpec(shape, lambda i, j: (0, 0))
    return pl.pallas_call(
        _xbc_dt_kernel,
        grid=(m // tm, SSD_XBC // tn),
        in_specs=[pl.BlockSpec((tm, k), lambda i, j: (i, 0)),
                  pl.BlockSpec((None, k, tn), lambda i, j: (layer, 0, P_XBC // tn + j)),
                  const((k, nd)), const((nd, k)), const((1, nd)), const((nd, 1))],
        out_specs=[pl.BlockSpec((tm, tn), lambda i, j: (i, j)),
                   pl.BlockSpec((2, tm, SSD_HEADS), lambda i, j: (0, i, 0)),
                   pl.BlockSpec((cpb, 2, SSD_HEADS, SSD_CHUNK), lambda i, j: (i, 0, 0, 0))],
        out_shape=[jax.ShapeDtypeStruct((m, SSD_XBC), BF16),
                   jax.ShapeDtypeStruct((2, m, SSD_HEADS), F32),
                   jax.ShapeDtypeStruct((m // SSD_CHUNK, 2, SSD_HEADS, SSD_CHUNK), F32)],
        compiler_params=_params(("parallel", "arbitrary")),
        name="proj_xbc_dt",
    )(h, w, w_dt_t.T, w_dt_t, bias.reshape(1, nd), bias.reshape(nd, 1))


def _dwconv_kernel(cur_ref, left_ref, right_ref, w_ref, b_ref, g_ref, beta_ref, o_ref, pad_ref, y_ref,
                   *, taps, tt, layer_norm):
    i = pl.program_id(1)
    n_i = pl.num_programs(1)
    nct = pad_ref.shape[0]
    lmask = (i > 0).astype(F32)
    rmask = (i < n_i - 1).astype(F32)
    for c in range(nct):
        sl = slice(c * LANES, (c + 1) * LANES)
        pad_ref[c, 0:HALO, :] = left_ref[0, :, sl].astype(F32) * lmask
        pad_ref[c, HALO:HALO + tt, :] = cur_ref[0, :, sl].astype(F32)
        pad_ref[c, HALO + tt:, :] = right_ref[0, :, sl].astype(F32) * rmask

    rows = 64
    first = HALO - taps // 2

    def tile_body(c, carry):
        accs = [jnp.broadcast_to(b_ref[c], (rows, LANES)) for _ in range(tt // rows)]
        for k in range(taps):
            wk = w_ref[c, k:k + 1, :]
            for rc in range(tt // rows):
                accs[rc] = accs[rc] + wk * pad_ref[c, pl.ds(first + k + rc * rows, rows), :]
        for rc in range(tt // rows):
            y_ref[c, rc * rows:(rc + 1) * rows, :] = accs[rc]
        return carry

    lax.fori_loop(0, nct, tile_body, 0)

    if layer_norm:
        tot = y_ref[0]
        for c in range(1, nct):
            tot = tot + y_ref[c]
        mu = jnp.sum(tot, axis=-1, keepdims=True) * (1.0 / (nct * LANES))
        sq = jnp.square(y_ref[0] - mu)
        for c in range(1, nct):
            sq = sq + jnp.square(y_ref[c] - mu)
        rstd = lax.rsqrt(jnp.sum(sq, axis=-1, keepdims=True) * (1.0 / (nct * LANES)) + NORM_EPS)
    for c in range(nct):
        sl = slice(c * LANES, (c + 1) * LANES)
        y = y_ref[c]
        if layer_norm:
            y = (y - mu) * rstd * g_ref[c] + beta_ref[c]
        o_ref[0, :, sl] = (y * _sigmoid(y)).astype(o_ref.dtype)


def _dwconv(x, w, b, g, beta, *, layer_norm, tt=256):
    bsz, t, c = x.shape
    taps = w.shape[0]
    nct = c // LANES
    w3 = w.reshape(taps, nct, LANES).transpose(1, 0, 2)
    tiles = lambda v: v.reshape(nct, 1, LANES)
    hb = tt // HALO
    n_halo = t // HALO
    small = lambda shape: pl.BlockSpec(shape, lambda bb, i: (0,) * len(shape))
    return pl.pallas_call(
        functools.partial(_dwconv_kernel, taps=taps, tt=tt, layer_norm=layer_norm),
        grid=(bsz, t // tt),
        in_specs=[pl.BlockSpec((1, tt, c), lambda bb, i: (bb, i, 0)),
                  pl.BlockSpec((1, HALO, c), lambda bb, i: (bb, jnp.maximum(i * hb - 1, 0), 0)),
                  pl.BlockSpec((1, HALO, c), lambda bb, i: (bb, jnp.minimum((i + 1) * hb, n_halo - 1), 0)),
                  small((nct, taps, LANES)), small((nct, 1, LANES)),
                  small((nct, 1, LANES)), small((nct, 1, LANES))],
        out_specs=pl.BlockSpec((1, tt, c), lambda bb, i: (bb, i, 0)),
        out_shape=jax.ShapeDtypeStruct((bsz, t, c), BF16),
        scratch_shapes=[pltpu.VMEM((nct, tt + 2 * HALO, LANES), F32), pltpu.VMEM((nct, tt, LANES), F32)],
        compiler_params=_params(("parallel", "parallel")),
        name="dwconv_ln" if layer_norm else "dwconv",
    )(x, x, x, w3, tiles(b), tiles(g), tiles(beta))


def _head_block_diag(v, lane):
    zero = jnp.zeros_like(v)
    return jnp.concatenate([jnp.where(lane < SSD_HEAD_DIM, v, zero),
                            jnp.where(lane >= SSD_HEAD_DIM, v, zero)], axis=0)


def _split3(v):
    hi = v.astype(BF16)
    r1 = v - hi.astype(F32)
    mid = r1.astype(BF16)
    lo = (r1 - mid.astype(F32)).astype(BF16)
    return jnp.concatenate([hi, mid, lo], axis=1)


def _ssd_kernel(xs_ref, b_ref, c_ref, dt_ref, dtt_ref, alog_ref, alogc_ref, y_ref, state_ref, neg_ref, *, cpb):
    direction = pl.program_id(1)
    fwd = direction == 0
    q = SSD_CHUNK

    @pl.when(pl.program_id(2) == 0)
    def _():
        state_ref[...] = jnp.zeros_like(state_ref)

    row = lax.broadcasted_iota(jnp.int32, (q, q), 0)
    col = lax.broadcasted_iota(jnp.int32, (q, q), 1)
    sign = jnp.where(fwd, 1, -1)
    mask = (row - col) * sign >= 0
    neg_ref[...] = jnp.where(mask, 0.0, -jnp.inf)
    tri = mask.astype(F32)
    tri_t = ((col - row) * sign >= 0).astype(F32)
    head_of_col = lax.broadcasted_iota(jnp.int32, (3 * SSD_HEADS, D_MODEL), 1) // SSD_HEAD_DIM
    head_of_row = lax.broadcasted_iota(jnp.int32, (3 * SSD_HEADS, D_MODEL), 0) % SSD_HEADS
    expand3 = jnp.where(head_of_col == head_of_row, 1.0, 0.0).astype(BF16)
    lane = lax.broadcasted_iota(jnp.int32, (q, LANES), 1)
    a_row = -jnp.exp(alog_ref[0])
    a_col = -jnp.exp(alogc_ref[0])
    hi = lax.Precision.HIGHEST

    for step in range(cpb):
        ci = jnp.where(fwd, step, cpb - 1 - step)
        r0 = pl.multiple_of(ci * q, q)
        x = xs_ref[0, pl.ds(r0, q), :]
        bm = b_ref[0, pl.ds(r0, q), :]
        cm = c_ref[0, pl.ds(r0, q), :]
        dt = dt_ref[0, 0, pl.ds(r0, q), :]
        dtt = dtt_ref[ci, 0]
        cs = jnp.dot(tri, dt * a_row, precision=hi, preferred_element_type=F32)
        cs_t = jnp.dot(dtt * a_col, tri_t, precision=hi, preferred_element_type=F32)
        tot = jnp.where(fwd, cs[q - 1:q, :], cs[0:1, :])
        scale3 = _split3(jnp.concatenate([dt * jnp.exp(tot - cs), jnp.exp(cs)], axis=0))
        cs2 = cs * LOG2E
        cs2_t = cs_t * LOG2E - jnp.log(dtt) * LOG2E
        for g in range(SSD_GROUPS):
            gs = slice(g * SSD_STATE, (g + 1) * SSD_STATE)
            xg = slice(g * GROUP_W, (g + 1) * GROUP_W)
            bg, cg = bm[:, gs], cm[:, gs]
            scale_x = jnp.dot(scale3, expand3[:, xg], preferred_element_type=F32)
            w_state = scale_x[:q]
            e_in = scale_x[q:]
            decay_row = jnp.where(fwd, e_in[q - 1:q, :], e_in[0:1, :])
            xw = (x[:, xg].astype(F32) * w_state).astype(BF16)
            state = state_ref[:, xg]
            cb = lax.dot_general(cg, bg, (((1,), (1,)), ((), ())), preferred_element_type=F32)
            y_off = jnp.dot(cg, state.astype(BF16), preferred_element_type=F32) * e_in
            bg_t = bg.astype(F32).T.astype(BF16)
            state_ref[:, xg] = state * decay_row + jnp.dot(bg_t, xw, preferred_element_type=F32)
            for pr in range(GROUP_W // LANES):
                t = g * (GROUP_W // LANES) + pr
                ts = slice(t * LANES, (t + 1) * LANES)
                ms = []
                for h in (2 * t, 2 * t + 1):
                    seg = (cs2[:, h:h + 1] - cs2_t[h:h + 1, :]) + neg_ref[...]
                    ms.append((cb * jnp.exp2(seg)).astype(BF16))
                y_diag = jnp.dot(jnp.concatenate(ms, axis=1), _head_block_diag(x[:, ts], lane),
                                 preferred_element_type=F32)
                y_ref[0, 0, pl.ds(r0, q), ts] = (y_diag + y_off[:, pr * LANES:(pr + 1) * LANES]).astype(y_ref.dtype)


def _ssd_scan(xbc_act, dt, dtt, a_log, rows=512):
    bsz, t, _ = xbc_act.shape
    nblk = t // rows
    cpb = rows // SSD_CHUNK
    blk = lambda d, i: jnp.where(d == 0, i, nblk - 1 - i)
    bc0 = D_MODEL // SSD_BC
    return pl.pallas_call(
        functools.partial(_ssd_kernel, cpb=cpb),
        grid=(bsz, 2, nblk),
        in_specs=[pl.BlockSpec((1, rows, D_MODEL), lambda b, d, i: (b, blk(d, i), 0)),
                  pl.BlockSpec((1, rows, SSD_BC), lambda b, d, i: (b, blk(d, i), bc0)),
                  pl.BlockSpec((1, rows, SSD_BC), lambda b, d, i: (b, blk(d, i), bc0 + 1)),
                  pl.BlockSpec((1, 1, rows, SSD_HEADS), lambda b, d, i: (d, b, blk(d, i), 0)),
                  pl.BlockSpec((cpb, 1, SSD_HEADS, SSD_CHUNK),
                               lambda b, d, i: (b * nblk + blk(d, i), d, 0, 0)),
                  pl.BlockSpec((1, 1, SSD_HEADS), lambda b, d, i: (d, 0, 0)),
                  pl.BlockSpec((1, SSD_HEADS, 1), lambda b, d, i: (d, 0, 0))],
        out_specs=pl.BlockSpec((1, 1, rows, D_MODEL), lambda b, d, i: (d, b, blk(d, i), 0)),
        out_shape=jax.ShapeDtypeStruct((2, bsz, t, D_MODEL), BF16),
        scratch_shapes=[pltpu.VMEM((SSD_STATE, D_MODEL), F32), pltpu.VMEM((SSD_CHUNK, SSD_CHUNK), F32)],
        compiler_params=_params(("parallel", "parallel", "arbitrary")),
        name="ssd_scan",
    )(xbc_act, xbc_act, xbc_act, dt.reshape(2, bsz, t, SSD_HEADS), dtt,
      a_log.reshape(2, 1, SSD_HEADS), a_log.reshape(2, SSD_HEADS, 1))


def _attn_kernel(q_ref, k_ref, v_ref, slope_ref, o_ref, lse_ref, *, sub, dilation, radius, qb, kw):
    def body(qi, carry):
        q0 = pl.multiple_of(qi * qb, qb)
        k0 = pl.multiple_of(jnp.clip(q0 - radius, 0, sub - kw), radius)
        qpos = q0 + lax.broadcasted_iota(jnp.int32, (qb, kw), 0)
        kpos = k0 + lax.broadcasted_iota(jnp.int32, (qb, kw), 1)
        dist = jnp.abs(kpos - qpos)
        valid = dist <= radius
        neg_dist = dist.astype(F32) * (-float(dilation))
        for h in range(ATTN_HPG):
            hs = slice(h * ATTN_HEAD_DIM, (h + 1) * ATTN_HEAD_DIM)
            q = q_ref[0, 0, pl.ds(q0, qb), hs]
            k = k_ref[0, 0, pl.ds(k0, kw), hs]
            v = v_ref[0, 0, pl.ds(k0, kw), hs]
            s = lax.dot_general(q, k, (((1,), (1,)), ((), ())), preferred_element_type=F32)
            s = jnp.where(valid, s + slope_ref[0, h:h + 1, :1] * neg_dist, NEG_INF)
            m = jnp.max(s, axis=-1, keepdims=True)
            p = jnp.exp(s - m)
            l = jnp.sum(p, axis=-1, keepdims=True)
            o = jnp.dot(p.astype(BF16), v, preferred_element_type=F32)
            o_ref[0, 0, pl.ds(q0, qb), hs] = (o / l).astype(o_ref.dtype)
            lse_ref[0, 0, pl.ds(q0, qb), hs] = jnp.broadcast_to(m + jnp.log(l), (qb, LANES))
        return carry

    lax.fori_loop(0, sub // qb, body, 0, unroll=min(4, sub // qb))


def _attention_group(qkv, slopes, group, window, dilation):
    bsz, _, sub, _ = qkv.shape
    radius = window // (2 * dilation)
    qb = min(2 * radius, sub)
    kw = min(qb + 2 * radius, sub)
    spec = lambda part: pl.BlockSpec((1, 1, sub, ATTN_OUT), lambda b, r: (b, r, 0, part))
    out_spec = pl.BlockSpec((1, 1, sub, ATTN_OUT), lambda b, r: (b, r, 0, 0))
    return pl.pallas_call(
        functools.partial(_attn_kernel, sub=sub, dilation=dilation, radius=radius, qb=qb, kw=kw),
        grid=(bsz, dilation),
        in_specs=[spec(0), spec(1), spec(2),
                  pl.BlockSpec((1, ATTN_HPG, LANES), lambda b, r: (group, 0, 0))],
        out_specs=[out_spec, out_spec],
        out_shape=[jax.ShapeDtypeStruct((bsz, dilation, sub, ATTN_OUT), BF16),
                   jax.ShapeDtypeStruct((bsz, dilation, sub, ATTN_OUT), F32)],
        compiler_params=_params(("parallel", "parallel")),
        name=f"attn_d{dilation}",
    )(qkv, qkv, qkv, slopes)


def _combine_kernel(yf_ref, yb_ref, xs_ref, sz_ref, d_ref, gs_ref,
                    o1_ref, o2_ref, o3_ref, l1_ref, l2_ref, l3_ref, ys_ref, ya_ref,
                    o2_scr, o3_scr, l2_scr, l3_scr):
    rows = 64
    for r0 in range(0, ys_ref.shape[0], rows):
        rs = slice(r0, r0 + rows)
        y = yf_ref[rs, :].astype(F32) + yb_ref[rs, :].astype(F32) + d_ref[...] * xs_ref[rs, :].astype(F32)
        y = y * sz_ref[rs, :].astype(F32)
        ys_ref[rs, :] = _rms(y, gs_ref[...]).astype(ys_ref.dtype)
    for src_o, src_l, dst_o, dst_l in ((o2_ref, l2_ref, o2_scr, l2_scr), (o3_ref, l3_ref, o3_scr, l3_scr)):
        dil, per = src_o.shape[1], src_o.shape[2]
        for r in range(dil):
            for hh in range(ATTN_HPG):
                hs = slice(hh * LANES, (hh + 1) * LANES)
                dst_o[hh, pl.ds(r, per, stride=dil), :] = src_o[0, r, :, hs].astype(F32)
                dst_l[hh, pl.ds(r, per, stride=dil), :] = src_l[0, r, :, hs]
    for hh in range(ATTN_HPG):
        hs = slice(hh * LANES, (hh + 1) * LANES)
        for r0 in range(0, ya_ref.shape[0], rows):
            rs = slice(r0, r0 + rows)
            l1, l2, l3 = l1_ref[0, 0, rs, hs], l2_scr[hh, rs, :], l3_scr[hh, rs, :]
            mx = jnp.maximum(jnp.maximum(l1, l2), l3)
            w1, w2, w3 = jnp.exp(l1 - mx), jnp.exp(l2 - mx), jnp.exp(l3 - mx)
            num = w1 * o1_ref[0, 0, rs, hs].astype(F32) + w2 * o2_scr[hh, rs, :] + w3 * o3_scr[hh, rs, :]
            ya_ref[rs, hs] = (num / (w1 + w2 + w3)).astype(ya_ref.dtype)


def _combine(y_dirs, xbc_act, zg, d_exp, g_ssd, att, t, tm=512):
    m = zg.shape[0]
    d = D_MODEL
    nt = t // tm
    row = pl.BlockSpec((tm, d), lambda i: (i, 0))
    direction = lambda dd: pl.BlockSpec((None, tm, d), lambda i: (dd, i, 0))
    vec = pl.BlockSpec((1, d), lambda i: (0, 0))
    att_spec = lambda dil: pl.BlockSpec((1, dil, tm // dil, ATTN_OUT), lambda i: (i // nt, 0, i % nt, 0))
    (o1, l1), (o2, l2), (o3, l3) = att
    dils = [dil for _, dil in ATTN_PATTERNS]
    return pl.pallas_call(
        _combine_kernel,
        grid=(m // tm,),
        in_specs=[direction(0), direction(1), row, row, vec, vec,
                  att_spec(dils[0]), att_spec(dils[1]), att_spec(dils[2]),
                  att_spec(dils[0]), att_spec(dils[1]), att_spec(dils[2])],
        out_specs=[row, pl.BlockSpec((tm, ATTN_OUT), lambda i: (i, 0))],
        out_shape=[jax.ShapeDtypeStruct((m, d), BF16), jax.ShapeDtypeStruct((m, ATTN_OUT), BF16)],
        scratch_shapes=[pltpu.VMEM((ATTN_HPG, tm, LANES), F32) for _ in range(4)],
        compiler_params=_params(("parallel",)),
        name="branch_combine",
    )(y_dirs, y_dirs, xbc_act, zg, d_exp, g_ssd, o1, o2, o3, l1, l2, l3)


def _branch_kernel(ys_ref, ya_ref, yc_ref, g1_ref, g2_ref, g3_ref, ws_ref, wa_ref, wc_ref, out_ref):
    for c in range(out_ref.shape[1] // CHUNK):
        cs = slice(c * CHUNK, (c + 1) * CHUNK)
        acc = g1_ref[:, cs].astype(F32) * jnp.dot(ys_ref[...], ws_ref[:, cs], preferred_element_type=F32)
        acc += g2_ref[:, cs].astype(F32) * jnp.dot(ya_ref[...], wa_ref[:, cs], preferred_element_type=F32)
        acc += g3_ref[:, cs].astype(F32) * jnp.dot(yc_ref[...], wc_ref[:, cs], preferred_element_type=F32)
        out_ref[:, cs] = acc.astype(out_ref.dtype)


def _branch_merge(ys, ya, yc, zg, w_s, w_a, w_c, layer, tm=1024, tn=512):
    m, d = ys.shape
    row = lambda w: pl.BlockSpec((tm, w), lambda i, j: (i, 0))
    gate = lambda g: pl.BlockSpec((tm, tn), lambda i, j: (i, (P_GATE + g * d) // tn + j))
    wspec = lambda k: pl.BlockSpec((None, k, tn), lambda i, j: (layer, 0, j))
    return pl.pallas_call(
        _branch_kernel,
        grid=(m // tm, d // tn),
        in_specs=[row(d), row(ATTN_OUT), row(d), gate(0), gate(1), gate(2),
                  wspec(d), wspec(ATTN_OUT), wspec(d)],
        out_specs=pl.BlockSpec((tm, tn), lambda i, j: (i, j)),
        out_shape=jax.ShapeDtypeStruct((m, d), BF16),
        compiler_params=_params(("parallel", "arbitrary")),
        name="branch_merge",
    )(ys, ya, yc, zg, zg, zg, w_s, w_a, w_c)


def _out_kernel(m_ref, w_ref, x_ref, g_ref, xo_ref, h_ref):
    for c in range(xo_ref.shape[1] // CHUNK):
        cs = slice(c * CHUNK, (c + 1) * CHUNK)
        xo_ref[:, cs] = x_ref[:, cs] + jnp.dot(m_ref[...], w_ref[:, cs], preferred_element_type=F32)
    h_ref[...] = _rms(xo_ref[...], g_ref[...]).astype(h_ref.dtype)


def _out_proj(merged, w_out, layer, x, g2, tm=512):
    m, d = x.shape
    row = pl.BlockSpec((tm, d), lambda i: (i, 0))
    return pl.pallas_call(
        _out_kernel,
        grid=(m // tm,),
        in_specs=[row, pl.BlockSpec((None, d, d), lambda i: (layer, 0, 0)), row,
                  pl.BlockSpec((1, d), lambda i: (0, 0))],
        out_specs=[row, row],
        out_shape=[jax.ShapeDtypeStruct((m, d), F32), jax.ShapeDtypeStruct((m, d), BF16)],
        compiler_params=_params(("parallel",)),
        name="out_proj",
    )(merged, w_out, x, g2.reshape(1, d))


def _mlp_kernel(h_ref, wu_ref, wd_ref, x_ref, g_ref, xo_ref, *maybe_hn_ref):
    j = pl.program_id(1)

    @pl.when(j == 0)
    def _():
        xo_ref[...] = x_ref[...]

    us = []
    for c in range(wu_ref.shape[1] // CHUNK):
        u = jnp.maximum(jnp.dot(h_ref[...], wu_ref[:, c * CHUNK:(c + 1) * CHUNK], preferred_element_type=F32), 0.0)
        us.append((u * u).astype(BF16))
    u = jnp.concatenate(us, axis=1)
    for c in range(xo_ref.shape[1] // CHUNK):
        cs = slice(c * CHUNK, (c + 1) * CHUNK)
        xo_ref[:, cs] += jnp.dot(u, wd_ref[:, cs], preferred_element_type=F32)

    if maybe_hn_ref:
        @pl.when(j == pl.num_programs(1) - 1)
        def _():
            maybe_hn_ref[0][...] = _rms(xo_ref[...], g_ref[...]).astype(maybe_hn_ref[0].dtype)


def _mlp(h2, w_up, w_down, layer, x, g_next, tm=512, th=1024):
    m, d = x.shape
    hid = w_up.shape[2]
    row = pl.BlockSpec((tm, d), lambda i, j: (i, 0))
    with_norm = g_next is not None
    gain = (g_next if with_norm else jnp.ones((d,), F32)).reshape(1, d)
    res = pl.pallas_call(
        _mlp_kernel,
        grid=(m // tm, hid // th),
        in_specs=[row, pl.BlockSpec((None, d, th), lambda i, j: (layer, 0, j)),
                  pl.BlockSpec((None, th, d), lambda i, j: (layer, j, 0)),
                  row, pl.BlockSpec((1, d), lambda i, j: (0, 0))],
        out_specs=[row, row] if with_norm else [row],
        out_shape=[jax.ShapeDtypeStruct((m, d), F32)] + ([jax.ShapeDtypeStruct((m, d), BF16)] if with_norm else []),
        compiler_params=_params(("parallel", "arbitrary")),
        name="mlp",
    )(h2, w_up, w_down, x, gain)
    return res if with_norm else (res[0], None)


PACK_W = 512


def _pack_kernel(tbl_ref, a_ref, b_ref, dt_in_ref, o_ref, dt_out_ref, *, first_shifted, shift):
    src = tbl_ref[pl.program_id(1)]
    dt_out_ref[...] = dt_in_ref[...].astype(dt_out_ref.dtype)

    @pl.when(src < first_shifted)
    def _():
        o_ref[...] = a_ref[...].T.astype(o_ref.dtype)

    @pl.when(src >= first_shifted)
    def _():
        rows = jnp.concatenate([a_ref[shift:, :], b_ref[...]], axis=0)
        o_ref[...] = rows.T.astype(o_ref.dtype)


def _pack_in_proj(w_t):
    depth, _, k = w_t.shape
    shift = COL_Q - COL_DT
    blk = lambda col, width: [col // PACK_W + b for b in range(width // PACK_W)]
    order = blk(COL_Z, D_MODEL) + blk(COL_GATE - shift, N_BRANCHES * D_MODEL) + blk(COL_GLU - shift, 2 * D_MODEL)
    order += blk(COL_XBC, SSD_XBC)
    for g in range(len(ATTN_PATTERNS)):
        for base in (COL_Q, COL_K, COL_V):
            order += blk(base - shift + g * ATTN_OUT, ATTN_OUT)
    table = jnp.asarray(order, jnp.int32)
    sub = PACK_W // shift
    return pl.pallas_call(
        functools.partial(_pack_kernel, first_shifted=COL_DT // PACK_W, shift=shift),
        grid_spec=pltpu.PrefetchScalarGridSpec(
            num_scalar_prefetch=1,
            grid=(depth, len(order)),
            in_specs=[pl.BlockSpec((None, PACK_W, k), lambda l, j, tbl: (l, tbl[j], 0)),
                      pl.BlockSpec((None, shift, k), lambda l, j, tbl: (l, (tbl[j] + 1) * sub, 0)),
                      pl.BlockSpec((None, shift, k), lambda l, j, tbl: (l, COL_DT // shift, 0))],
            out_specs=[pl.BlockSpec((None, k, PACK_W), lambda l, j, tbl: (l, 0, j)),
                       pl.BlockSpec((None, shift, k), lambda l, j, tbl: (l, 0, 0))]),
        out_shape=[jax.ShapeDtypeStruct((depth, k, len(order) * PACK_W), BF16),
                   jax.ShapeDtypeStruct((depth, shift, k), BF16)],
        compiler_params=_params(("arbitrary", "arbitrary")),
        name="pack_w_in",
    )(table, w_t, w_t, w_t)


def kernel(x, norm1_g, w_in, ssd_conv_w, ssd_conv_b, ssd_dt_bias, ssd_a_log, ssd_d, ssd_norm_g, w_ssd_o,
           q_norm_g, k_norm_g, w_attn_o, conv_dw_w, conv_dw_b, conv_ln_g, conv_ln_b, w_conv_o, w_out,
           norm2_g, w_mlp_up, w_mlp_down):
    bsz, t, d = x.shape
    m = bsz * t
    depth = w_in.shape[0]
    xf = x.reshape(m, d)
    slopes = jnp.exp2(-ALIBI_MAX_EXP * jnp.arange(1, ATTN_HEADS + 1, dtype=F32) / ATTN_HEADS)
    slopes = jnp.broadcast_to(slopes.reshape(len(ATTN_PATTERNS), ATTN_HPG, 1),
                              (len(ATTN_PATTERNS), ATTN_HPG, LANES))
    ones_c = jnp.ones((SSD_XBC,), F32)
    zeros_c = jnp.zeros((SSD_XBC,), F32)
    w, w_dt_t = _pack_in_proj(jnp.swapaxes(w_in, 1, 2))
    w_so, w_ao, w_co, w_o = (v.astype(BF16) for v in (w_ssd_o, w_attn_o, w_conv_o, w_out))
    w_up, w_down = w_mlp_up.astype(BF16), w_mlp_down.astype(BF16)
    h = _norm(xf, norm1_g[0])
    for i in range(depth):
        xbc, dt, dtt = _xbc_dt_proj(h, w, i, w_dt_t[i], ssd_dt_bias[i].reshape(-1))
        xbc_act = _dwconv(xbc.reshape(bsz, t, SSD_XBC), ssd_conv_w[i], ssd_conv_b[i], ones_c, zeros_c,
                          layer_norm=False)
        glu = _mm(h, w, i, [P_GLU_A, P_GLU_G], D_MODEL, _epi_glu, 1024, 1024, "proj_glu")
        yc = _dwconv(glu.reshape(bsz, t, D_MODEL), conv_dw_w[i], conv_dw_b[i], conv_ln_g[i], conv_ln_b[i],
                     layer_norm=True)
        zg = _mm(h, w, i, [P_Z], P_GLU_A, functools.partial(_epi_z_gates, n_silu_tiles=D_MODEL // 1024),
                 1024, 1024, "proj_z_gates")
        qk_gain = jnp.concatenate([jnp.tile(q_norm_g[i] * ATTN_HEAD_DIM ** -0.5, ATTN_HPG),
                                   jnp.tile(k_norm_g[i], ATTN_HPG),
                                   jnp.ones((ATTN_OUT,), F32)]).reshape(1, 3 * ATTN_OUT)
        att = []
        for g, (window, dilation) in enumerate(ATTN_PATTERNS):
            qkv = _qkv_proj(h, w, i, P_QKV + g * 3 * ATTN_OUT, qk_gain, bsz, t, dilation)
            att.append(_attention_group(qkv, slopes, g, window, dilation))
        y_dirs = _ssd_scan(xbc_act, dt, dtt, ssd_a_log[i])

        ys, ya = _combine(y_dirs.reshape(2, m, d), xbc_act.reshape(m, SSD_XBC), zg,
                          jnp.repeat(ssd_d[i], SSD_HEAD_DIM).reshape(1, d), ssd_norm_g[i].reshape(1, d), att, t)
        merged = _branch_merge(ys, ya, yc.reshape(m, d), zg, w_so, w_ao, w_co, i)
        xf, h2 = _out_proj(merged, w_o, i, xf, norm2_g[i])
        xf, h = _mlp(h2, w_up, w_down, i, xf, norm1_g[i + 1] if i + 1 < depth else None)
    return xf.reshape(bsz, t, d)
```
